```python
import functools
import jax, jax.numpy as jnp
from jax import lax
import numpy as np

D_MODEL = 2048
BATCH = 8
SEQ = 2048
DEPTH = 1
DEC_BATCH = 32
DEC_SEQ = 4
PAST_LEN = 16384
PAGE_SIZE = 128

HEAD_DIM = 64
MIX_WIDTH = D_MODEL
FOX_WIDTH = MIX_WIDTH // 2
RW_WIDTH = MIX_WIDTH - FOX_WIDTH
FOX_HEADS = FOX_WIDTH // HEAD_DIM
RW_HEADS = RW_WIDTH // HEAD_DIM
DECAY_LORA = 64
ICLR_LORA = 64
GATE_LORA = 128
N_META = 16
Q_BLOCK = 128
PEER_HEADS = 8
PEER_N_KEYS = 128
PEER_N_EXPERTS = PEER_N_KEYS * PEER_N_KEYS
PEER_KEY_DIM = 128
PEER_HALF = PEER_KEY_DIM // 2
PEER_TOPK = 16
PEER_CHUNK = 128
RMS_EPS = 1e-6
GN_EPS = HEAD_DIM * 1e-5
NEG_BIG = -1e30
FOX_SIZES = (FOX_WIDTH, FOX_WIDTH, FOX_WIDTH, FOX_HEADS, FOX_WIDTH)
RW_SIZES = (RW_WIDTH, RW_WIDTH, RW_WIDTH, DECAY_LORA, ICLR_LORA, GATE_LORA)
FOX_COLS = sum(FOX_SIZES)
RW_COLS = sum(RW_SIZES)
IN_COLS = FOX_COLS + RW_COLS

kernel_name = 'hymba_rwkv7_fox_peer_step'


def _split(p, sizes):
    offs = np.cumsum(sizes)[:-1].tolist()
    return jnp.split(p, offs, axis=-1)


def _rmsnorm(x, w):
    xf = x.astype(jnp.float32)
    return xf * lax.rsqrt(jnp.mean(xf * xf, axis=-1, keepdims=True) + RMS_EPS) * w.astype(jnp.float32)


def _rel_cumsum(logf):
    return logf - lax.cumsum(logf, axis=1, reverse=True)


def _online_softmax_step(carry, s, vblk):
    m, l, acc = carry
    m_new = jnp.maximum(m, s.max(-1))
    p = jnp.exp(s - m_new[..., None])
    corr = jnp.exp(m - m_new)
    return (m_new, l * corr + p.sum(-1), acc * corr[..., None] + jnp.einsum('bhts,bshd->bhtd', p, vblk))


def _fox_prompt_attention(q, k, v, logf):
    B, L = q.shape[:2]
    cT = jnp.swapaxes(_rel_cumsum(logf), 1, 2)
    pos = jnp.arange(L)

    def attend(qb, cqb, pb):
        s = jnp.einsum('bqhd,bkhd->bhqk', qb, k) + cqb[..., None] - cT[:, :, None, :]
        s = jnp.where(pos[None, :] <= pb[:, None], s, NEG_BIG)
        return jnp.einsum('bhqk,bkhd->bqhd', jax.nn.softmax(s, axis=-1), v)

    o_meta = attend(q[:, :N_META], cT[:, :, :N_META], pos[:N_META])
    nb = (L - N_META) // Q_BLOCK
    qr = jnp.swapaxes(q[:, N_META:].reshape(B, nb, Q_BLOCK, FOX_HEADS, HEAD_DIM), 0, 1)
    cr = cT[:, :, N_META:].reshape(B, FOX_HEADS, nb, Q_BLOCK).transpose(2, 0, 1, 3)
    pr = pos[N_META:].reshape(nb, Q_BLOCK)
    o_real = lax.map(lambda blk: attend(blk[0], blk[1], blk[2]), (qr, cr, pr))
    o_real = jnp.swapaxes(o_real, 0, 1).reshape(B, L - N_META, FOX_HEADS, HEAD_DIM)
    return jnp.concatenate([o_meta, o_real], axis=1)


def _fox_decode_attention(q, k, v, logf, cache_k, cache_v, cache_logf, page_table, layer):
    f32 = jnp.float32
    Bd, T = q.shape[:2]
    n_pages = page_table.shape[1]
    ps = cache_k.shape[2]
    logf_past = cache_logf[layer, page_table].astype(f32).reshape(Bd, n_pages * ps, FOX_HEADS)
    c = _rel_cumsum(jnp.concatenate([logf_past, logf], axis=1))
    c_past = c[:, :-T].reshape(Bd, n_pages, ps, FOX_HEADS).transpose(1, 0, 3, 2)
    c_new = jnp.swapaxes(c[:, -T:], 1, 2)

    def page_step(carry, inp):
        pt, cp = inp
        kp = cache_k[layer, pt].astype(f32)
        vp = cache_v[layer, pt].astype(f32)
        s = jnp.einsum('bthd,bshd->bhts', q, kp) + c_new[..., None] - cp[:, :, None, :]
        return _online_softmax_step(carry, s, vp), None

    init = (jnp.full((Bd, FOX_HEADS, T), NEG_BIG, f32), jnp.zeros((Bd, FOX_HEADS, T), f32),
            jnp.zeros((Bd, FOX_HEADS, T, HEAD_DIM), f32))
    carry, _ = lax.scan(page_step, init, (page_table.T, c_past))
    s = jnp.einsum('bthd,bshd->bhts', q, k) + c_new[..., None] - c_new[:, :, None, :]
    s = jnp.where(jnp.tril(jnp.ones((T, T), dtype=bool)), s, NEG_BIG)
    _, l, acc = _online_softmax_step(carry, s, v)
    return jnp.swapaxes(acc / l[..., None], 1, 2)


def _fox_project(p_fx, fox_bf, fox_qn, fox_kn):
    B, L, _ = p_fx.shape
    q, k, v, fl, og = _split(p_fx.astype(jnp.float32), FOX_SIZES)
    heads = lambda t: t.reshape(B, L, FOX_HEADS, HEAD_DIM)
    q = _rmsnorm(heads(q), fox_qn) * (HEAD_DIM ** -0.5)
    k = _rmsnorm(heads(k), fox_kn)
    logf = jax.nn.log_sigmoid(fl + fox_bf)
    return q, k, heads(v), logf, og


def _fox_output(o, og, fox_on):
    B, L = o.shape[:2]
    o = _rmsnorm(o, fox_on.reshape(FOX_HEADS, HEAD_DIM)).reshape(B, L, FOX_WIDTH)
    return o * jax.nn.sigmoid(og)


def _rwkv7_mix(p_rw, shift_prev, S0, rw_mu, rw_w0, rw_w2, rw_a0, rw_a2, rw_g2, rw_kk, rw_ka, rw_rk, rw_ln_w, rw_ln_b):
    f32 = jnp.float32
    B, L, _ = p_rw.shape
    p_rw = p_rw.astype(f32)
    prev = jnp.concatenate([shift_prev.astype(f32)[:, None, :], p_rw[:, :-1]], axis=1)
    ps = p_rw + rw_mu * (prev - p_rw)
    r, k, v, wd, ad, gd = _split(ps, RW_SIZES)
    w_log = -jax.nn.softplus(-(rw_w0 + jnp.tanh(wd) @ rw_w2)) - 0.5
    decay = jnp.exp(-jnp.exp(w_log.astype(f32)))
    a = jax.nn.sigmoid(rw_a0 + ad @ rw_a2)
    g = jax.nn.sigmoid(gd) @ rw_g2
    heads = lambda t: t.astype(f32).reshape(B, L, RW_HEADS, HEAD_DIM)
    r, k, v, decay, a = heads(r), heads(k), heads(v), heads(decay), heads(a)
    kk = k * rw_kk.reshape(RW_HEADS, HEAD_DIM)
    kk = kk * lax.rsqrt(jnp.sum(kk * kk, axis=-1, keepdims=True) + 1e-12)
    k = k * (1.0 + (a - 1.0) * rw_ka.reshape(RW_HEADS, HEAD_DIM))
    xs = tuple(jnp.swapaxes(t, 0, 1) for t in (r, decay, k, v, -kk, kk * a))

    def step(S, inp):
        r_t, w_t, k_t, v_t, ka_t, kb_t = inp
        sa = jnp.einsum('bhvk,bhk->bhv', S, ka_t)
        S = S * w_t[:, :, None, :] + sa[..., None] * kb_t[:, :, None, :] + v_t[..., None] * k_t[:, :, None, :]
        return S, jnp.einsum('bhvk,bhk->bhv', S, r_t)

    S_fin, y = lax.scan(step, S0.astype(f32), xs)
    y = jnp.swapaxes(y, 0, 1)
    mean = y.mean(-1, keepdims=True)
    var = jnp.mean(jnp.square(y - mean), axis=-1, keepdims=True)
    y = ((y - mean) * lax.rsqrt(var + GN_EPS)).reshape(B, L, RW_WIDTH) * rw_ln_w + rw_ln_b
    bonus = (jnp.sum(r * k * rw_rk, axis=-1, keepdims=True) * v).reshape(B, L, RW_WIDTH)
    return (y + bonus) * g, S_fin, p_rw[:, -1]


def _peer(xn, peer_wq, peer_keys, peer_u, peer_v):
    f32 = jnp.float32
    n = xn.shape[0]
    pad = (-n) % PEER_CHUNK
    xc = jnp.pad(xn, ((0, pad), (0, 0))).reshape(-1, PEER_CHUNK, D_MODEL)
    keys = peer_keys.astype(f32)

    def chunk(xb):
        q = (xb @ peer_wq).astype(f32).reshape(PEER_CHUNK, PEER_HEADS, 2, PEER_HALF)
        s = jnp.einsum('chpd,hpkd->chpk', q, keys)
        sv, si = lax.top_k(s, PEER_TOPK)
        cand = (sv[:, :, 0, :, None] + sv[:, :, 1, None, :]).reshape(PEER_CHUNK, PEER_HEADS, PEER_TOPK * PEER_TOPK)
        cidx = (si[:, :, 0, :, None] * PEER_N_KEYS + si[:, :, 1, None, :]).reshape(PEER_CHUNK, PEER_HEADS, PEER_TOPK * PEER_TOPK)
        fv, fi = lax.top_k(cand, PEER_TOPK)
        eidx = jnp.take_along_axis(cidx, fi, axis=-1)
        gate = jax.nn.softmax(fv, axis=-1)
        ue = peer_u[eidx].astype(f32)
        ve = peer_v[eidx].astype(f32)
        h = jnp.einsum('cd,chkd->chk', xb, ue)
        return jnp.einsum('chk,chkd->cd', gate * jax.nn.gelu(h, approximate=False), ve)

    return lax.map(chunk, xc).reshape(-1, D_MODEL)[:n]


def _trunk_layer(x, shift_prev, wkv_prev, attend, norm1_w, w_in, fox_bf, fox_qn, fox_kn, fox_on,
                 rw_mu, rw_w0, rw_w2, rw_a0, rw_a2, rw_g2, rw_kk, rw_ka, rw_rk, rw_ln_w, rw_ln_b,
                 w_o, norm2_w, peer_wq, peer_keys, peer_u, peer_v):
    xn = _rmsnorm(x, norm1_w)
    p = xn @ w_in
    p_fx, p_rw = p[..., :FOX_COLS], p[..., FOX_COLS:]
    q, k, v, logf, og = _fox_project(p_fx, fox_bf, fox_qn, fox_kn)
    fox_o = _fox_output(attend(q, k, v, logf), og, fox_on)
    rw_o, wkv, shift = _rwkv7_mix(p_rw, shift_prev, wkv_prev, rw_mu, rw_w0, rw_w2, rw_a0, rw_a2, rw_g2,
                                  rw_kk, rw_ka, rw_rk, rw_ln_w, rw_ln_b)
    h = x + jnp.concatenate([fox_o, rw_o], axis=-1) @ w_o
    B, L, _ = h.shape
    hn = _rmsnorm(h, norm2_w).reshape(B * L, D_MODEL)
    y = h + _peer(hn, peer_wq, peer_keys, peer_u, peer_v).reshape(B, L, D_MODEL)
    return y, k, v, logf, wkv, shift


def setup_inputs(seed: int = 0) -> dict:
    key = jax.random.key(seed)
    ks = jax.random.split(key, 40)
    f32 = jnp.float32
    nrm = lambda k, shape: jax.random.normal(k, shape, f32)
    n_pages = PAST_LEN // PAGE_SIZE
    n_used = DEC_BATCH * n_pages
    n_pool = n_used + n_used // 4
    page_table = jax.random.permutation(ks[0], n_pool)[:n_used].reshape(DEC_BATCH, n_pages).astype(jnp.int32)
    return {
        'x_prompt': nrm(ks[1], (BATCH, SEQ, D_MODEL)),
        'x_sample': nrm(ks[2], (DEC_BATCH, DEC_SEQ, D_MODEL)),
        'cache_k': nrm(ks[3], (DEPTH, n_pool, PAGE_SIZE, FOX_HEADS, HEAD_DIM)),
        'cache_v': nrm(ks[4], (DEPTH, n_pool, PAGE_SIZE, FOX_HEADS, HEAD_DIM)),
        'cache_logf': jax.nn.log_sigmoid(nrm(ks[5], (DEPTH, n_pool, PAGE_SIZE, FOX_HEADS)) + 8.0),
        'state_wkv': nrm(ks[6], (DEPTH, DEC_BATCH, RW_HEADS, HEAD_DIM, HEAD_DIM)) * 0.5,
        'state_shift': nrm(ks[7], (DEPTH, DEC_BATCH, RW_COLS)),
        'page_table': page_table,
        'meta_tokens': nrm(ks[8], (N_META, D_MODEL)),
        'norm1_w': 1.0 + 0.02 * nrm(ks[9], (DEPTH, D_MODEL)),
        'w_in': nrm(ks[10], (DEPTH, D_MODEL, IN_COLS)) * D_MODEL ** -0.5,
        'fox_bf': 6.0 + 0.5 * nrm(ks[11], (DEPTH, FOX_HEADS)),
        'fox_qn': 1.0 + 0.02 * nrm(ks[12], (DEPTH, HEAD_DIM)),
        'fox_kn': 1.0 + 0.02 * nrm(ks[13], (DEPTH, HEAD_DIM)),
        'fox_on': 1.0 + 0.02 * nrm(ks[14], (DEPTH, FOX_WIDTH)),
        'rw_mu': jax.random.uniform(ks[15], (DEPTH, RW_COLS), f32),
        'rw_w0': 0.5 * nrm(ks[16], (DEPTH, RW_WIDTH)),
        'rw_w2': 0.1 * nrm(ks[17], (DEPTH, DECAY_LORA, RW_WIDTH)),
        'rw_a0': 0.5 * nrm(ks[18], (DEPTH, RW_WIDTH)),
        'rw_a2': 0.1 * nrm(ks[19], (DEPTH, ICLR_LORA, RW_WIDTH)),
        'rw_g2': nrm(ks[20], (DEPTH, GATE_LORA, RW_WIDTH)) * GATE_LORA ** -0.5,
        'rw_kk': 0.85 + 0.05 * nrm(ks[21], (DEPTH, RW_WIDTH)),
        'rw_ka': 1.0 + 0.05 * nrm(ks[22], (DEPTH, RW_WIDTH)),
        'rw_rk': 0.1 * nrm(ks[23], (DEPTH, RW_HEADS, HEAD_DIM)),
        'rw_ln_w': 1.0 + 0.02 * nrm(ks[24], (DEPTH, RW_WIDTH)),
        'rw_ln_b': 0.02 * nrm(ks[25], (DEPTH, RW_WIDTH)),
        'w_o': nrm(ks[26], (DEPTH, MIX_WIDTH, D_MODEL)) * MIX_WIDTH ** -0.5,
        'norm2_w': 1.0 + 0.02 * nrm(ks[27], (DEPTH, D_MODEL)),
        'peer_wq': nrm(ks[28], (DEPTH, D_MODEL, PEER_HEADS * PEER_KEY_DIM)) * D_MODEL ** -0.5,
        'peer_keys': nrm(ks[29], (DEPTH, PEER_HEADS, 2, PEER_N_KEYS, PEER_HALF)) * PEER_HALF ** -0.5,
        'peer_u': nrm(ks[30], (DEPTH, PEER_N_EXPERTS, D_MODEL)) * D_MODEL ** -0.5,
        'peer_v': nrm(ks[31], (DEPTH, PEER_N_EXPERTS, D_MODEL)) * PEER_HEADS ** -0.5,
    }


def reference(x_prompt, x_sample, cache_k, cache_v, cache_logf, state_wkv, state_shift, page_table,
              meta_tokens, norm1_w, w_in, fox_bf, fox_qn, fox_kn, fox_on,
              rw_mu, rw_w0, rw_w2, rw_a0, rw_a2, rw_g2, rw_kk, rw_ka, rw_rk, rw_ln_w, rw_ln_b,
              w_o, norm2_w, peer_wq, peer_keys, peer_u, peer_v):
    f32 = jnp.float32
    batch = x_prompt.shape[0]
    meta = jnp.broadcast_to(meta_tokens.astype(f32)[None], (batch, N_META, D_MODEL))
    hp = jnp.concatenate([meta, x_prompt.astype(f32)], axis=1)
    hs = x_sample.astype(f32)
    kp_l, vp_l, lfp_l, wkvp_l, shp_l = [], [], [], [], []
    ks_l, vs_l, lfs_l, wkvs_l, shs_l = [], [], [], [], []
    for l in range(DEPTH):
        lw = (norm1_w[l], w_in[l], fox_bf[l], fox_qn[l], fox_kn[l], fox_on[l],
              rw_mu[l], rw_w0[l], rw_w2[l], rw_a0[l], rw_a2[l], rw_g2[l], rw_kk[l], rw_ka[l], rw_rk[l],
              rw_ln_w[l], rw_ln_b[l], w_o[l], norm2_w[l], peer_wq[l], peer_keys[l], peer_u[l], peer_v[l])
        hp, kp, vp, lfp, wkvp, shp = _trunk_layer(
            hp, jnp.zeros((batch, RW_COLS), f32), jnp.zeros((batch, RW_HEADS, HEAD_DIM, HEAD_DIM), f32),
            _fox_prompt_attention, *lw)
        decode = functools.partial(_fox_decode_attention, cache_k=cache_k, cache_v=cache_v,
                                   cache_logf=cache_logf, page_table=page_table, layer=l)
        hs, ksm, vsm, lfs, wkvs, shs = _trunk_layer(hs, state_shift[l], state_wkv[l], decode, *lw)
        kp_l.append(kp); vp_l.append(vp); lfp_l.append(lfp); wkvp_l.append(wkvp); shp_l.append(shp)
        ks_l.append(ksm); vs_l.append(vsm); lfs_l.append(lfs); wkvs_l.append(wkvs); shs_l.append(shs)
    y_prompt = hp[:, N_META:]
    y_sample = hs
    return (y_prompt, y_sample,
            jnp.stack(kp_l), jnp.stack(vp_l), jnp.stack(lfp_l), jnp.stack(wkvp_l), jnp.stack(shp_l),
            jnp.stack(ks_l), jnp.stack(vs_l), jnp.stack(lfs_l), jnp.stack(wkvs_l), jnp.stack(shs_l))
```

```python
import functools

import numpy as np
import jax
import jax.numpy as jnp
from jax import lax
from jax.experimental import pallas as pl
from jax.experimental.pallas import tpu as pltpu

F32 = jnp.float32
BF16 = jnp.bfloat16

LANES = 128
HEAD_DIM = 64
N_META = 16
PEER_HEADS = 8
PEER_N_KEYS = 128
PEER_TOPK = 16
RMS_EPS = 1e-6
GN_EPS = HEAD_DIM * 1e-5
NEG_BIG = -1e30
SEQ_BLOCK = 128
FRONT_PAD = SEQ_BLOCK - N_META
VMEM_LIMIT = 48 * 1024 * 1024


def _cparams(*sem):
    return pltpu.CompilerParams(dimension_semantics=sem, vmem_limit_bytes=VMEM_LIMIT)


def _dot(a, b):
    return jnp.dot(a, b, preferred_element_type=F32)


def _dot_nt(a, b):
    return lax.dot_general(a, b, (((1,), (1,)), ((), ())), preferred_element_type=F32)


def _split2(x):
    hi = x.astype(BF16)
    lo = (x - hi.astype(F32)).astype(BF16)
    return hi, lo


def _split3(x):
    hi = x.astype(BF16)
    r = x - hi.astype(F32)
    mid = r.astype(BF16)
    lo = (r - mid.astype(F32)).astype(BF16)
    return hi, mid, lo


def _segsum(x, g):
    hi, lo = _split2(x)
    return _dot(hi, g) + _dot(lo, g)


def _log_sigmoid(z):
    return jnp.minimum(z, 0.0) - jnp.log1p(jnp.exp(-jnp.abs(z)))


def _head_ones(n):
    i = np.arange(n) // HEAD_DIM
    return jnp.asarray(i[:, None] == i[None, :], dtype=BF16)


def _inproj_kernel(x_ref, nw_ref, w_ref, cvec_ref, g_ref, o_ref, xn_ref, *, n_norm, j_fl):
    j = pl.program_id(1)

    @pl.when(j == 0)
    def _():
        x = x_ref[...]
        ms = jnp.mean(x * x, axis=-1, keepdims=True)
        xn_ref[...] = (x * lax.rsqrt(ms + RMS_EPS) * nw_ref[...]).astype(BF16)

    acc = _dot(xn_ref[...], w_ref[...])
    vec = cvec_ref[0]

    @pl.when(j < n_norm)
    def _():
        ss = _segsum(acc * acc, g_ref[...])
        o_ref[...] = acc * lax.rsqrt(ss * (1.0 / HEAD_DIM) + RMS_EPS) * vec

    @pl.when(j == j_fl)
    def _():
        o_ref[...] = _log_sigmoid(acc + vec)

    @pl.when(jnp.logical_and(j >= n_norm, j != j_fl))
    def _():
        o_ref[...] = acc


def _inproj(x, nw, w, cvec, g, *, tm, tn, n_norm, j_fl):
    rows, d = x.shape
    cols = w.shape[1]
    return pl.pallas_call(
        functools.partial(_inproj_kernel, n_norm=n_norm, j_fl=j_fl),
        grid=(rows // tm, cols // tn),
        in_specs=[
            pl.BlockSpec((tm, d), lambda i, j: (i, 0)),
            pl.BlockSpec((1, d), lambda i, j: (0, 0)),
            pl.BlockSpec((d, tn), lambda i, j: (0, j)),
            pl.BlockSpec((1, 1, tn), lambda i, j: (j, 0, 0)),
            pl.BlockSpec((tn, tn), lambda i, j: (0, 0)),
        ],
        out_specs=pl.BlockSpec((tm, tn), lambda i, j: (i, j)),
        out_shape=jax.ShapeDtypeStruct((rows, cols), F32),
        scratch_shapes=[pltpu.VMEM((tm, d), BF16)],
        compiler_params=_cparams("arbitrary", "arbitrary"),
        name="inproj",
    )(x, nw, w, cvec, g)


def _fcum_kernel(lf_ref, ccol_ref, crow_ref, *, nblk):
    r = lax.broadcasted_iota(jnp.int32, (SEQ_BLOCK, SEQ_BLOCK), 0)
    c = lax.broadcasted_iota(jnp.int32, (SEQ_BLOCK, SEQ_BLOCK), 1)
    upper = (c > r).astype(BF16)
    lower = (r > c).astype(BF16)

    def body(n, carry):
        ccar, rcar = carry
        i = nblk - 1 - n
        off = pl.multiple_of(i * SEQ_BLOCK, SEQ_BLOCK)
        x = lf_ref[0, pl.ds(off, SEQ_BLOCK), :]
        h, m, l = _split3(x)
        suf = _dot(upper, h) + _dot(upper, m) + _dot(upper, l)
        ccol_ref[0, pl.ds(off, SEQ_BLOCK), :] = -(suf + ccar)
        xt = x.T
        ht, mt, lt = _split3(xt)
        suft = _dot(ht, lower) + _dot(mt, lower) + _dot(lt, lower)
        crow_ref[0, :, pl.ds(off, SEQ_BLOCK)] = -(suft + rcar)
        return (ccar + suf[0:1, :] + x[0:1, :], rcar + suft[:, 0:1] + xt[:, 0:1])

    lax.fori_loop(0, nblk, body, (jnp.zeros((1, LANES), F32), jnp.zeros((LANES, 1), F32)))


def _fcum(p3, col_block):
    b, lp, _ = p3.shape
    return pl.pallas_call(
        functools.partial(_fcum_kernel, nblk=lp // SEQ_BLOCK),
        grid=(b,),
        in_specs=[pl.BlockSpec((1, lp, LANES), lambda i: (i, 0, col_block))],
        out_specs=[pl.BlockSpec((1, lp, LANES), lambda i: (i, 0, 0)),
                   pl.BlockSpec((1, LANES, lp), lambda i: (i, 0, 0))],
        out_shape=[jax.ShapeDtypeStruct((b, lp, LANES), F32), jax.ShapeDtypeStruct((b, LANES, lp), F32)],
        compiler_params=_cparams("arbitrary"),
        name="fcum",
    )(p3)


def _attn_kernel(q_ref, k_ref, v_ref, ccol_ref, crow_ref, o_ref):
    qi = pl.program_id(2)
    tq = SEQ_BLOCK
    lane = lax.broadcasted_iota(jnp.int32, (tq, LANES), 1)
    row = lax.broadcasted_iota(jnp.int32, (tq, LANES), 0)
    lo = lane < HEAD_DIM
    q = q_ref[0]
    qh = (jnp.where(lo, q, 0.0).astype(BF16), jnp.where(lo, 0.0, q).astype(BF16))
    cq = (jnp.broadcast_to(ccol_ref[0, 0, :, 0:1], (tq, LANES)),
          jnp.broadcast_to(ccol_ref[0, 0, :, 1:2], (tq, LANES)))

    def block(start, carry, mask):
        kb = k_ref[0, pl.ds(start, tq), :].astype(BF16)
        vb = v_ref[0, pl.ds(start, tq), :].astype(BF16)
        out = []
        for h in range(2):
            m, l, acc = carry[h]
            s = _dot_nt(qh[h], kb) + (cq[h] - crow_ref[0, 0, pl.ds(h, 1), pl.ds(start, tq)])
            if mask is not None:
                s = jnp.where(mask, s, NEG_BIG)
            m_new = jnp.maximum(m, jnp.max(s, axis=-1, keepdims=True))
            p = jnp.exp(s - m_new)
            corr = jnp.exp(m - m_new)
            out.append((m_new, l * corr + jnp.sum(p, axis=-1, keepdims=True),
                        acc * corr + _dot(p.astype(BF16), vb)))
        return tuple(out)

    init = tuple((jnp.full((tq, 1), NEG_BIG, F32), jnp.zeros((tq, 1), F32), jnp.zeros((tq, LANES), F32))
                 for _ in range(2))
    carry = block(0, init, lane >= FRONT_PAD)
    carry = lax.fori_loop(1, qi + 1,
                          lambda kb, c: block(pl.multiple_of(kb * tq, tq), c, None), carry)
    carry = block(pl.multiple_of((qi + 1) * tq, tq), carry, lane <= row)
    o_ref[0] = jnp.where(lo, carry[0][2] / carry[0][1], carry[1][2] / carry[1][1])


def _attention(p3, ccol, crow, *, nq, kcol, vcol):
    b, lp, _ = p3.shape
    nhp = ccol.shape[1]
    return pl.pallas_call(
        _attn_kernel,
        grid=(b, nhp, nq),
        in_specs=[
            pl.BlockSpec((1, SEQ_BLOCK, LANES), lambda i, h, q: (i, q + 1, h)),
            pl.BlockSpec((1, lp, LANES), lambda i, h, q: (i, 0, kcol + h)),
            pl.BlockSpec((1, lp, LANES), lambda i, h, q: (i, 0, vcol + h)),
            pl.BlockSpec((1, 1, SEQ_BLOCK, 2), lambda i, h, q: (i, h, q + 1, 0)),
            pl.BlockSpec((1, 1, 2, lp), lambda i, h, q: (i, h, 0, 0)),
        ],
        out_specs=pl.BlockSpec((1, SEQ_BLOCK, LANES), lambda i, h, q: (i, q, h)),
        out_shape=jax.ShapeDtypeStruct((b, nq * SEQ_BLOCK, nhp * LANES), F32),
        compiler_params=_cparams("arbitrary", "arbitrary", "arbitrary"),
        name="attn",
    )(p3, p3, p3, ccol, crow)


def _decattn_kernel(pt_ref, qbd_ref, knew_ref, vnew_ref, lnew_ref, ck_ref, cv_ref, cl_ref, o_ref,
                    m_ref, l_ref, acc_ref, car_ref, qt_ref, *, n_new, n_heads):
    step = pl.program_id(1)
    tq = SEQ_BLOCK
    tpad = tq // n_heads
    r = lax.broadcasted_iota(jnp.int32, (tq, tq), 0)
    c = lax.broadcasted_iota(jnp.int32, (tq, tq), 1)
    lower = (r > c).astype(BF16)
    er = lax.broadcasted_iota(jnp.int32, (tq, n_heads), 0)
    ec = lax.broadcasted_iota(jnp.int32, (tq, n_heads), 1)
    expand = (er // tpad == ec).astype(BF16)

    def process(kpage, vpage, lpt, is_new):
        a, b_, c_ = _split3(lpt)
        lpe = _dot(expand, a) + _dot(expand, b_) + _dot(expand, c_)
        a, b_, c_ = _split3(lpe)
        suf = _dot(a, lower) + _dot(b_, lower) + _dot(c_, lower)
        if is_new:
            tcol = r % tpad
            qt = -jnp.sum(jnp.where(c == tcol, suf, 0.0), axis=-1, keepdims=True)
            qt_ref[...] = qt
            car = jnp.zeros((tq, 1), F32)
            m = jnp.full((tq, 1), NEG_BIG, F32)
            l = jnp.zeros((tq, 1), F32)
        else:
            qt = qt_ref[...]
            car = car_ref[...]
            m = m_ref[...]
            l = l_ref[...]
        s = _dot_nt(qbd_ref[0], kpage.astype(BF16)) + suf + (car + qt)
        if is_new:
            s = jnp.where(jnp.logical_and(c <= tcol, c < n_new), s, NEG_BIG)
        m_new = jnp.maximum(m, jnp.max(s, axis=-1, keepdims=True))
        p = jnp.exp(s - m_new)
        corr = jnp.exp(m - m_new)
        pv = _dot(p.astype(BF16), vpage.astype(BF16))
        if is_new:
            acc_ref[...] = pv
        else:
            acc_ref[...] = acc_ref[...] * corr + pv
        m_ref[...] = m_new
        l_ref[...] = l * corr + jnp.sum(p, axis=-1, keepdims=True)
        car_ref[...] = car + suf[:, 0:1] + lpe[:, 0:1]

    @pl.when(step == 0)
    def _():
        process(knew_ref[0], vnew_ref[0], lnew_ref[0], True)

    @pl.when(step > 0)
    def _():
        process(ck_ref[0], cv_ref[0], cl_ref[0], False)

    @pl.when(step == pl.num_programs(1) - 1)
    def _():
        inv = 1.0 / l_ref[...]
        lo = lax.broadcasted_iota(jnp.int32, (tpad, LANES), 1) < HEAD_DIM
        for j in range(n_heads // 2):
            cols = slice(j * LANES, (j + 1) * LANES)
            ra = slice(2 * j * tpad, (2 * j + 1) * tpad)
            rb = slice((2 * j + 1) * tpad, (2 * j + 2) * tpad)
            o_ref[0, :, cols] = jnp.where(lo, acc_ref[ra, cols] * inv[ra], acc_ref[rb, cols] * inv[rb])


def _decode_attention(page_table, qbd, knew, vnew, lnew, ck, cv, cl, *, n_new, n_heads):
    bd, n_pages = page_table.shape
    width = ck.shape[-1]
    tpad = SEQ_BLOCK // n_heads

    def page(i, s, pt):
        return (pt[i, n_pages - jnp.maximum(s, 1)], 0, 0)

    grid_spec = pltpu.PrefetchScalarGridSpec(
        num_scalar_prefetch=1,
        grid=(bd, n_pages + 1),
        in_specs=[
            pl.BlockSpec((1, SEQ_BLOCK, width), lambda i, s, pt: (i, 0, 0)),
            pl.BlockSpec((1, SEQ_BLOCK, width), lambda i, s, pt: (i, 0, 0)),
            pl.BlockSpec((1, SEQ_BLOCK, width), lambda i, s, pt: (i, 0, 0)),
            pl.BlockSpec((1, n_heads, SEQ_BLOCK), lambda i, s, pt: (i, 0, 0)),
            pl.BlockSpec((1, SEQ_BLOCK, width), page),
            pl.BlockSpec((1, SEQ_BLOCK, width), page),
            pl.BlockSpec((1, n_heads, SEQ_BLOCK), page),
        ],
        out_specs=pl.BlockSpec((1, tpad, width), lambda i, s, pt: (i, 0, 0)),
        scratch_shapes=[pltpu.VMEM((SEQ_BLOCK, 1), F32), pltpu.VMEM((SEQ_BLOCK, 1), F32),
                        pltpu.VMEM((SEQ_BLOCK, width), F32), pltpu.VMEM((SEQ_BLOCK, 1), F32),
                        pltpu.VMEM((SEQ_BLOCK, 1), F32)],
    )
    return pl.pallas_call(
        functools.partial(_decattn_kernel, n_new=n_new, n_heads=n_heads),
        grid_spec=grid_spec,
        out_shape=jax.ShapeDtypeStruct((bd, tpad, width), F32),
        compiler_params=_cparams("arbitrary", "arbitrary"),
        name="decattn",
    )(page_table, qbd, knew, vnew, lnew, ck, cv, cl)


def _rwprep_kernel(*refs, seq_len, has_start):
    n_in = 4 + (4 if has_start else 0)
    p_refs = refs[0:4]
    st_refs = refs[4:8] if has_start else None
    (mur, muk, muv, mul, w0, w2, a0, a2, g2, kkw, kaw, rkw, g_ref) = refs[n_in:n_in + 13]
    (r_o, w_o, k_o, v_o, nkk_o, kka_o, gate_o, bon_o) = refs[n_in + 13:n_in + 21]
    carry = refs[n_in + 21:n_in + 25]
    i = pl.program_id(0)
    tm = p_refs[0].shape[0]

    @pl.when(i == 0)
    def _():
        for cr in carry:
            cr[...] = jnp.zeros_like(cr)

    mixed = []
    for n, (pr, mu) in enumerate(zip(p_refs, (mur, muk, muv, mul))):
        p = pr[...]
        row = lax.broadcasted_iota(jnp.int32, p.shape, 0)
        prev = jnp.where(row == 0, carry[n][0:1, :], pltpu.roll(p, 1, axis=0))
        if has_start:
            prev = jnp.where(row % seq_len == 0, st_refs[n][...], prev)
        carry[n][0:1, :] = p[tm - 1:tm, :]
        mixed.append(p + mu[...] * (prev - p))
    r, k, v, lora = mixed
    gm = g_ref[...]
    wa = lora[:, 0:LANES]
    wlog = _log_sigmoid(w0[...] + _dot(jnp.tanh(wa).astype(BF16), w2[...])) - 0.5
    decay = jnp.exp(-jnp.exp(wlog))
    a = jax.nn.sigmoid(a0[...] + _dot(wa.astype(BF16), a2[...]))
    gate = _dot(jax.nn.sigmoid(lora[:, LANES:]).astype(BF16), g2[...])
    kk = k * kkw[...]
    kk = kk * lax.rsqrt(_segsum(kk * kk, gm) + 1e-12)
    k2 = k * (1.0 + (a - 1.0) * kaw[...])
    r_o[...] = r
    w_o[...] = decay
    k_o[...] = k2
    v_o[...] = v
    nkk_o[...] = -kk
    kka_o[...] = kk * a
    gate_o[...] = gate
    bon_o[...] = _segsum(r * k2 * rkw[...], gm) * v


def _rwprep(p2, starts, params, gm, *, tm, seq_len, rcol, lcol):
    rows = p2.shape[0]
    wid = gm.shape[0]
    lw = 2 * LANES
    has_start = starts is not None
    row_spec = lambda w, cb: pl.BlockSpec((tm, w), lambda i: (i, cb))
    in_specs = [row_spec(wid, rcol), row_spec(wid, rcol + 1), row_spec(wid, rcol + 2), row_spec(lw, lcol)]
    args = [p2, p2, p2, p2]
    if has_start:
        in_specs += [row_spec(wid, 0), row_spec(wid, 0), row_spec(wid, 0), row_spec(lw, 0)]
        args += list(starts)
    for prm in params:
        in_specs.append(pl.BlockSpec(prm.shape, lambda i: (0, 0)))
    in_specs.append(pl.BlockSpec(gm.shape, lambda i: (0, 0)))
    args += list(params) + [gm]
    return pl.pallas_call(
        functools.partial(_rwprep_kernel, seq_len=seq_len, has_start=has_start),
        grid=(rows // tm,),
        in_specs=in_specs,
        out_specs=[pl.BlockSpec((tm, wid), lambda i: (i, 0))] * 8,
        out_shape=[jax.ShapeDtypeStruct((rows, wid), F32)] * 8,
        scratch_shapes=[pltpu.VMEM((8, wid), F32)] * 3 + [pltpu.VMEM((8, lw), F32)],
        compiler_params=_cparams("arbitrary"),
        name="rwprep",
    )(*args)


def _scan_kernel(r_ref, w_ref, k_ref, v_ref, a_ref, b_ref, s0_ref, y_ref, sT_ref, s_ref, *, tb):
    tblk = pl.program_id(1)
    dk = s_ref.shape[0]

    @pl.when(tblk == 0)
    def _():
        s_ref[...] = s0_ref[...]

    def step(t, carry):
        sa = s_ref[0] * a_ref[t, pl.ds(0, 1), :]
        for kx in range(1, dk):
            sa = sa + s_ref[kx] * a_ref[t, pl.ds(kx, 1), :]
        vt = v_ref[t]
        y = None
        for kx in range(dk):
            sk = (s_ref[kx] * w_ref[t, pl.ds(kx, 1), :] + sa * b_ref[t, pl.ds(kx, 1), :]
                  + vt * k_ref[t, pl.ds(kx, 1), :])
            s_ref[kx] = sk
            yk = sk * r_ref[t, pl.ds(kx, 1), :]
            y = yk if y is None else y + yk
        y_ref[t] = y
        return carry

    lax.fori_loop(0, tb, step, 0)

    @pl.when(tblk == pl.num_programs(1) - 1)
    def _():
        sT_ref[...] = s_ref[...]


def _scan(seqs, s0, *, tb, t_off, n_steps):
    npairs = s0.shape[-1]
    dk, dv = s0.shape[0], s0.shape[1]
    seq_spec = pl.BlockSpec((tb, dk, LANES), lambda g, t: (t + t_off // tb, 0, g))
    st_spec = pl.BlockSpec((dk, dv, LANES), lambda g, t: (0, 0, g))
    return pl.pallas_call(
        functools.partial(_scan_kernel, tb=tb),
        grid=(npairs // LANES, n_steps // tb),
        in_specs=[seq_spec] * 6 + [st_spec],
        out_specs=[pl.BlockSpec((tb, dv, LANES), lambda g, t: (t, 0, g)), st_spec],
        out_shape=[jax.ShapeDtypeStruct((n_steps, dv, npairs), F32), jax.ShapeDtypeStruct(s0.shape, F32)],
        scratch_shapes=[pltpu.VMEM((dk, dv, LANES), F32)],
        compiler_params=_cparams("arbitrary", "arbitrary"),
        name="wkvscan",
    )(*seqs, s0)


def _mix_kernel(x_ref, o_ref, og_ref, y_ref, gate_ref, bon_ref, on_ref, lnw_ref, lnb_ref, n2_ref,
                wo_ref, g_ref, h_ref, hn_ref):
    gm = g_ref[...]
    half = o_ref.shape[1]
    o = o_ref[...]
    fox = (o * lax.rsqrt(_segsum(o * o, gm) * (1.0 / HEAD_DIM) + RMS_EPS) * on_ref[...]
           * jax.nn.sigmoid(og_ref[...]))
    y = y_ref[...]
    d = y - _segsum(y, gm) * (1.0 / HEAD_DIM)
    var = _segsum(d * d, gm) * (1.0 / HEAD_DIM)
    rw = (d * lax.rsqrt(var + GN_EPS) * lnw_ref[...] + lnb_ref[...] + bon_ref[...]) * gate_ref[...]
    h = (x_ref[...] + _dot(fox.astype(BF16), wo_ref[0:half, :]) + _dot(rw.astype(BF16), wo_ref[half:, :]))
    h_ref[...] = h
    ms = jnp.mean(h * h, axis=-1, keepdims=True)
    hn_ref[...] = h * lax.rsqrt(ms + RMS_EPS) * n2_ref[...]


def _mix(x, o, og_src, y, gate_src, bon_src, vecs, wo, gm, *, n_blocks, og_map, rw_map):
    tm = SEQ_BLOCK
    d = x.shape[1]
    half = d // 2
    blk = lambda w, f: pl.BlockSpec((tm, w), f)
    const = lambda a: pl.BlockSpec(a.shape, lambda i: (0, 0))
    return pl.pallas_call(
        _mix_kernel,
        grid=(n_blocks,),
        in_specs=[blk(d, lambda i: (i, 0)), blk(half, lambda i: (i, 0)), blk(half, og_map),
                  blk(half, lambda i: (i, 0)), blk(half, rw_map), blk(half, rw_map)]
        + [const(v) for v in vecs] + [const(wo), const(gm)],
        out_specs=[blk(d, lambda i: (i, 0))] * 2,
        out_shape=[jax.ShapeDtypeStruct((n_blocks * tm, d), F32)] * 2,
        compiler_params=_cparams("arbitrary"),
        name="mix",
    )(x, o, og_src, y, gate_src, bon_src, *vecs, wo, gm)


def _topk_rows(s_ref, n_rows, out_v, out_i, idx_ref=None):
    pos = lax.broadcasted_iota(jnp.int32, (n_rows, s_ref.shape[1]), 0)
    for t in range(PEER_TOPK):
        s = s_ref[...]
        m = jnp.max(s, axis=0, keepdims=True)
        am = jnp.min(jnp.where(s == m, pos, n_rows), axis=0, keepdims=True)
        hit = pos == am
        out_v[pl.ds(t, 1), :] = m
        if idx_ref is None:
            out_i[pl.ds(t, 1), :] = am
        else:
            out_i[pl.ds(t, 1), :] = jnp.max(jnp.where(hit, idx_ref[...], -1), axis=0, keepdims=True)
        s_ref[...] = jnp.where(hit, -jnp.inf, s)


def _peersel_kernel(hn_ref, wq_ref, keys_ref, eidx_ref, gate_ref, s_ref, sv_ref, si_ref, cand_ref, cidx_ref,
                    fv_ref, fi_ref):
    q = _dot(hn_ref[...].astype(BF16), wq_ref[...])
    for h in range(PEER_HEADS):
        qh = q[:, h * LANES:(h + 1) * LANES].astype(BF16)
        for half in range(2):
            s_ref[...] = _dot_nt(keys_ref[h, half], qh)
            _topk_rows(s_ref, PEER_N_KEYS, sv_ref.at[half], si_ref.at[half])
        sv1 = sv_ref[1]
        si1 = si_ref[1]
        for i in range(PEER_TOPK):
            rows = slice(i * PEER_TOPK, (i + 1) * PEER_TOPK)
            cand_ref[rows, :] = sv_ref[0, pl.ds(i, 1), :] + sv1
            cidx_ref[rows, :] = si_ref[0, pl.ds(i, 1), :] * PEER_N_KEYS + si1
        _topk_rows(cand_ref, PEER_TOPK * PEER_TOPK, fv_ref, fi_ref, idx_ref=cidx_ref)
        fv = fv_ref[...]
        e = jnp.exp(fv - fv[0:1, :])
        rows = slice(h * PEER_TOPK, (h + 1) * PEER_TOPK)
        gate_ref[rows, :] = e / jnp.sum(e, axis=0, keepdims=True)
        eidx_ref[rows, :] = fi_ref[...]


def _peer_select(hn, wq, keys):
    n, d = hn.shape
    tm = SEQ_BLOCK
    nsel = PEER_HEADS * PEER_TOPK
    ncand = PEER_TOPK * PEER_TOPK
    return pl.pallas_call(
        _peersel_kernel,
        grid=(n // tm,),
        in_specs=[pl.BlockSpec((tm, d), lambda i: (i, 0)),
                  pl.BlockSpec(wq.shape, lambda i: (0, 0)),
                  pl.BlockSpec(keys.shape, lambda i: (0, 0, 0, 0))],
        out_specs=[pl.BlockSpec((nsel, tm), lambda i: (0, i))] * 2,
        out_shape=[jax.ShapeDtypeStruct((nsel, n), jnp.int32), jax.ShapeDtypeStruct((nsel, n), F32)],
        scratch_shapes=[pltpu.VMEM((PEER_N_KEYS, tm), F32),
                        pltpu.VMEM((2, PEER_TOPK, tm), F32), pltpu.VMEM((2, PEER_TOPK, tm), jnp.int32),
                        pltpu.VMEM((ncand, tm), F32), pltpu.VMEM((ncand, tm), jnp.int32),
                        pltpu.VMEM((PEER_TOPK, tm), F32), pltpu.VMEM((PEER_TOPK, tm), jnp.int32)],
        compiler_params=_cparams("arbitrary"),
        name="peersel",
    )(hn, wq, keys)


def _peer_kernel(eidx_hbm, table_hbm, x_ref, gate_ref, hres_ref, out_ref, idx_smem, buf, orow, sem_idx, sems,
                 *, tb, nsel, rows_per_expert):
    i = pl.program_id(0)
    rpe = rows_per_expert
    nchunk = rpe // 2
    slot_rows = nsel * rpe
    cp = pltpu.make_async_copy(eidx_hbm.at[pl.ds(pl.multiple_of(i * (tb * nsel), tb * nsel), tb * nsel)],
                               idx_smem, sem_idx)
    cp.start()
    cp.wait()

    def issue(t, slot):
        for kx in range(nsel):
            e = idx_smem[t * nsel + kx]
            pltpu.make_async_copy(table_hbm.at[pl.ds(pl.multiple_of(e * rpe, rpe), rpe), :],
                                  buf.at[slot, pl.ds(kx * rpe, rpe), :], sems.at[slot]).start()

    def wait(slot):
        pltpu.make_async_copy(table_hbm.at[pl.ds(0, slot_rows), :], buf.at[slot], sems.at[slot]).wait()

    lane = lax.broadcasted_iota(jnp.int32, gate_ref.shape, 1)
    group = 8

    def compute(base, sub, slot):
        t = base + sub
        acc = None
        for cx in range(nchunk):
            u = buf[slot, pl.ds(cx, nsel, stride=rpe), :]
            xg = x_ref[pl.ds(base, group), cx * LANES:(cx + 1) * LANES]
            term = u * xg[sub:sub + 1, :]
            acc = term if acc is None else acc + term
        hid = jnp.sum(acc, axis=-1, keepdims=True)
        gate = jnp.sum(jnp.where(lane == t, gate_ref[...], 0.0), axis=-1, keepdims=True)
        act = gate * (hid * (lax.erf(hid / np.sqrt(2.0).astype(np.float32)) + 1.0) / 2.0)
        actb = jnp.broadcast_to(act, (nsel, LANES))
        for cx in range(nchunk):
            v = buf[slot, pl.ds(nchunk + cx, nsel, stride=rpe), :]
            orow[sub:sub + 1, cx * LANES:(cx + 1) * LANES] = jnp.sum(v * actb, axis=0, keepdims=True)

    issue(0, 0)

    def octet(j, carry):
        base = pl.multiple_of(j * group, group)
        for sub in range(group):
            slot = sub % 2
            if sub + 1 < group:
                issue(base + sub + 1, 1 - slot)
            else:
                @pl.when(base + group < tb)
                def _():
                    issue(base + group, 1 - slot)
            wait(slot)
            compute(base, sub, slot)
        out_ref[pl.ds(base, group), :] = hres_ref[pl.ds(base, group), :] + orow[...]
        return carry

    lax.fori_loop(0, tb // group, octet, 0)


def _peer_combine(eidx_flat, table, hn, gate_t, hres, *, nsel, rows_per_expert):
    n, d = hn.shape
    tb = SEQ_BLOCK
    return pl.pallas_call(
        functools.partial(_peer_kernel, tb=tb, nsel=nsel, rows_per_expert=rows_per_expert),
        grid=(n // tb,),
        in_specs=[pl.BlockSpec(memory_space=pl.ANY),
                  pl.BlockSpec(memory_space=pl.ANY),
                  pl.BlockSpec((tb, d), lambda i: (i, 0)),
                  pl.BlockSpec((nsel, tb), lambda i: (0, i)),
                  pl.BlockSpec((tb, d), lambda i: (i, 0))],
        out_specs=pl.BlockSpec((tb, d), lambda i: (i, 0)),
        out_shape=jax.ShapeDtypeStruct((n, d), F32),
        scratch_shapes=[pltpu.SMEM((tb * nsel,), jnp.int32),
                        pltpu.VMEM((2, nsel * rows_per_expert, LANES), F32),
                        pltpu.VMEM((8, d), F32),
                        pltpu.SemaphoreType.DMA(()),
                        pltpu.SemaphoreType.DMA((2,))],
        compiler_params=_cparams("arbitrary"),
        name="peer",
    )(eidx_flat, table, hn, gate_t, hres)


def _to_pairs(a, b, l):
    nh = a.shape[1] // HEAD_DIM
    a = a.reshape(b, l, nh, HEAD_DIM).transpose(1, 3, 0, 2).reshape(l, HEAD_DIM, b * nh)
    pad = (-a.shape[-1]) % LANES
    return jnp.pad(a, ((0, 0), (0, 0), (0, pad))) if pad else a


def kernel(x_prompt, x_sample, cache_k, cache_v, cache_logf, state_wkv, state_shift, page_table, meta_tokens,
           norm1_w, w_in, fox_bf, fox_qn, fox_kn, fox_on, rw_mu, rw_w0, rw_w2, rw_a0, rw_a2, rw_g2, rw_kk,
           rw_ka, rw_rk, rw_ln_w, rw_ln_b, w_o, norm2_w, peer_wq, peer_keys, peer_u, peer_v):
    depth = w_in.shape[0]
    assert depth == 1, "single-layer step"
    bsz, seq, d = x_prompt.shape
    bd, tdec, _ = x_sample.shape
    half = d // 2
    nh = half // HEAD_DIM
    nhp = half // LANES
    lp = FRONT_PAD + N_META + seq
    nq = seq // SEQ_BLOCK
    lora_w = 2 * LANES
    assert seq % SEQ_BLOCK == 0 and rw_w2.shape[1] + rw_a2.shape[1] == LANES and rw_g2.shape[1] == LANES

    tn = lora_w
    w = w_in[0]
    fox_cols = 4 * half + nh
    wq_, wk_, wv_ = w[:, 0:half], w[:, half:2 * half], w[:, 2 * half:3 * half]
    wfl, wog = w[:, 3 * half:3 * half + nh], w[:, 3 * half + nh:fox_cols]
    wrw = w[:, fox_cols:]
    w_all = jnp.concatenate([wq_, wk_, wv_, wog, wrw, jnp.pad(wfl, ((0, 0), (0, tn - nh)))], axis=1).astype(BF16)
    ncol = w_all.shape[1]
    nblk = ncol // tn
    n_norm = 2 * half // tn
    j_fl = nblk - 1
    rep = tn // HEAD_DIM
    cvec = jnp.zeros((nblk, tn), F32)
    cvec = cvec.at[0:n_norm // 2].set(jnp.tile(fox_qn[0] * (HEAD_DIM ** -0.5), rep))
    cvec = cvec.at[n_norm // 2:n_norm].set(jnp.tile(fox_kn[0], rep))
    cvec = cvec.at[j_fl, 0:nh].set(fox_bf[0]).reshape(nblk, 1, tn)
    g_tn = _head_ones(tn)
    g_half = _head_ones(half)
    nw1 = norm1_w[0].reshape(1, d)
    col_k, col_v, col_og, col_rw = half, 2 * half, 3 * half, 4 * half
    col_lora = col_rw + 3 * half
    col_fl = col_lora + lora_w

    mu = rw_mu[0]
    row = lambda a: a.reshape(1, -1)
    zpad = jnp.zeros((LANES - rw_w2.shape[1], half), F32)
    rw_params = [row(mu[0:half]), row(mu[half:2 * half]), row(mu[2 * half:3 * half]), row(mu[3 * half:]),
                 row(rw_w0[0]), jnp.concatenate([rw_w2[0], zpad], 0).astype(BF16),
                 row(rw_a0[0]), jnp.concatenate([zpad, rw_a2[0]], 0).astype(BF16),
                 rw_g2[0].astype(BF16), row(rw_kk[0]), row(rw_ka[0]), row(rw_rk[0])]
    mix_vecs = [row(fox_on[0]), row(rw_ln_w[0]), row(rw_ln_b[0]), row(norm2_w[0])]
    wo = w_o[0].astype(BF16)
    wq_peer = peer_wq[0].astype(BF16)
    khalf = peer_keys.shape[-1]
    keys = peer_keys[0].astype(BF16)
    keys = jnp.stack([jnp.pad(keys[:, 0], ((0, 0), (0, 0), (0, LANES - khalf))),
                      jnp.pad(keys[:, 1], ((0, 0), (0, 0), (khalf, LANES - 2 * khalf))) ], axis=1)
    rpe = 2 * d // LANES
    table = jnp.concatenate([peer_u[0], peer_v[0]], axis=1).reshape(-1, LANES)
    nsel = PEER_HEADS * PEER_TOPK

    def peer(hn, hres):
        eidx_t, gate_t = _peer_select(hn, wq_peer, keys)
        return _peer_combine(eidx_t.T.reshape(-1), table, hn, gate_t, hres, nsel=nsel, rows_per_expert=rpe)

    hp = jnp.concatenate([jnp.zeros((bsz, FRONT_PAD, d), F32),
                          jnp.broadcast_to(meta_tokens.astype(F32)[None], (bsz, N_META, d)),
                          x_prompt.astype(F32)], axis=1).reshape(bsz * lp, d)
    tm_p = 1024 if (bsz * lp) % 1024 == 0 else SEQ_BLOCK
    pp = _inproj(hp, nw1, w_all, cvec, g_tn, tm=tm_p, tn=tn, n_norm=n_norm, j_fl=j_fl)
    pp3 = pp.reshape(bsz, lp, ncol)
    ccol, crow = _fcum(pp3, col_fl // LANES)
    ccol = ccol[:, :, 0:nh].reshape(bsz, lp, nhp, 2).transpose(0, 2, 1, 3)
    crow = crow[:, 0:nh, :].reshape(bsz, nhp, 2, lp)
    o_p = _attention(pp3, ccol, crow, nq=nq, kcol=col_k // LANES, vcol=col_v // LANES)
    o_p = o_p.reshape(bsz * seq, half)

    rw_p = _rwprep(pp, None, rw_params, g_half, tm=SEQ_BLOCK, seq_len=lp,
                   rcol=col_rw // half, lcol=col_lora // lora_w)
    seqs_p = [_to_pairs(a, bsz, lp) for a in rw_p[0:6]]
    npair_p = seqs_p[0].shape[-1]
    tb_p = N_META
    y_p, s_p = _scan(seqs_p, jnp.zeros((HEAD_DIM, HEAD_DIM, npair_p), F32),
                     tb=tb_p, t_off=FRONT_PAD, n_steps=N_META + seq)
    y_p = y_p[N_META:, :, 0:bsz * nh].reshape(seq, HEAD_DIM, bsz, nh).transpose(2, 0, 3, 1).reshape(bsz * seq, half)
    wkv_p = s_p[:, :, 0:bsz * nh].transpose(2, 1, 0).reshape(1, bsz, nh, HEAD_DIM, HEAD_DIM)

    blocks_per_seq = lp // SEQ_BLOCK
    in_rows = lambda i: (i // nq) * blocks_per_seq + 1 + i % nq
    h_p, hn_p = _mix(x_prompt.reshape(bsz * seq, d).astype(F32), o_p, pp, y_p, rw_p[6], rw_p[7], mix_vecs, wo,
                     g_half, n_blocks=bsz * nq,
                     og_map=lambda i: (in_rows(i), col_og // half), rw_map=lambda i: (in_rows(i), 0))
    y_prompt = peer(hn_p, h_p).reshape(bsz, seq, d)

    pr = pp3[:, FRONT_PAD:]
    k_prompt = pr[:, :, col_k:col_k + half].reshape(1, bsz, N_META + seq, nh, HEAD_DIM)
    v_prompt = pr[:, :, col_v:col_v + half].reshape(1, bsz, N_META + seq, nh, HEAD_DIM)
    logf_prompt = pr[:, :, col_fl:col_fl + nh].reshape(1, bsz, N_META + seq, nh)
    shift_prompt = pp3[:, -1, col_rw:col_lora + lora_w].reshape(1, bsz, -1)

    nrow_d = bd * tdec
    rows_d = -(-nrow_d // SEQ_BLOCK) * SEQ_BLOCK
    xs = x_sample.astype(F32).reshape(nrow_d, d)
    xs_pad = jnp.pad(xs, ((0, rows_d - nrow_d), (0, 0)))
    pd_ = _inproj(xs_pad, nw1, w_all, cvec, g_tn, tm=SEQ_BLOCK, tn=tn, n_norm=n_norm, j_fl=j_fl)
    pd3 = pd_[0:nrow_d].reshape(bd, tdec, ncol)
    tpad = SEQ_BLOCK // nh
    q4 = pd3[:, :, 0:half].reshape(bd, tdec, nh, HEAD_DIM)
    qbd = jnp.eye(nh, dtype=F32)[None, :, None, :, None] * q4[:, None]
    qbd = jnp.pad(qbd, ((0, 0), (0, 0), (0, tpad - tdec), (0, 0), (0, 0))).reshape(bd, SEQ_BLOCK, half).astype(BF16)
    padk = ((0, 0), (0, SEQ_BLOCK - tdec), (0, 0))
    knew = jnp.pad(pd3[:, :, col_k:col_k + half], padk)
    vnew = jnp.pad(pd3[:, :, col_v:col_v + half], padk)
    lnew = jnp.pad(pd3[:, :, col_fl:col_fl + nh].transpose(0, 2, 1), ((0, 0), (0, 0), (0, SEQ_BLOCK - tdec)))
    n_pool, page_size = cache_k.shape[1], cache_k.shape[2]
    assert page_size == SEQ_BLOCK
    o_d = _decode_attention(page_table, qbd, knew, vnew, lnew,
                            cache_k[0].astype(F32).reshape(n_pool, page_size, half),
                            cache_v[0].astype(F32).reshape(n_pool, page_size, half),
                            cache_logf[0].astype(F32).transpose(0, 2, 1), n_new=tdec, n_heads=nh)
    o_d = jnp.pad(o_d[:, 0:tdec].reshape(nrow_d, half), ((0, rows_d - nrow_d), (0, 0)))

    sh = state_shift[0].astype(F32)
    st = jnp.pad(jnp.repeat(sh, tdec, axis=0), ((0, rows_d - nrow_d), (0, 0)))
    starts = [st[:, 0:half], st[:, half:2 * half], st[:, 2 * half:3 * half], st[:, 3 * half:]]
    rw_d = _rwprep(pd_, starts, rw_params, g_half, tm=SEQ_BLOCK, seq_len=tdec,
                   rcol=col_rw // half, lcol=col_lora // lora_w)
    seqs_d = [_to_pairs(a[0:nrow_d], bd, tdec) for a in rw_d[0:6]]
    s0_d = state_wkv[0].astype(F32).reshape(bd * nh, HEAD_DIM, HEAD_DIM).transpose(2, 1, 0)
    pad_d = (-s0_d.shape[-1]) % LANES
    if pad_d:
        s0_d = jnp.pad(s0_d, ((0, 0), (0, 0), (0, pad_d)))
    y_d, s_d = _scan(seqs_d, s0_d, tb=tdec, t_off=0, n_steps=tdec)
    y_d = y_d[:, :, 0:bd * nh].reshape(tdec, HEAD_DIM, bd, nh).transpose(2, 0, 3, 1).reshape(nrow_d, half)
    y_d = jnp.pad(y_d, ((0, rows_d - nrow_d), (0, 0)))
    wkv_d = s_d[:, :, 0:bd * nh].transpose(2, 1, 0).reshape(1, bd, nh, HEAD_DIM, HEAD_DIM)
    h_d, hn_d = _mix(xs_pad, o_d, pd_, y_d, rw_d[6], rw_d[7], mix_vecs, wo, g_half, n_blocks=rows_d // SEQ_BLOCK,
                     og_map=lambda i: (i, col_og // half), rw_map=lambda i: (i, 0))
    y_sample = peer(hn_d, h_d)[0:nrow_d].reshape(bd, tdec, d)

    k_sample = pd3[:, :, col_k:col_k + half].reshape(1, bd, tdec, nh, HEAD_DIM)
    v_sample = pd3[:, :, col_v:col_v + half].reshape(1, bd, tdec, nh, HEAD_DIM)
    logf_sample = pd3[:, :, col_fl:col_fl + nh].reshape(1, bd, tdec, nh)
    shift_sample = pd3[:, -1, col_rw:col_lora + lora_w].reshape(1, bd, -1)

    return (y_prompt, y_sample, k_prompt, v_prompt, logf_prompt, wkv_p, shift_prompt,
            k_sample, v_sample, logf_sample, wkv_d, shift_sample)
```

```python
import functools

import numpy as np
import jax
import jax.numpy as jnp
from jax import lax
from jax.experimental import pallas as pl
from jax.experimental.pallas import tpu as pltpu

F32 = jnp.float32
BF16 = jnp.bfloat16

LANES = 128
HEAD_DIM = 64
N_META = 16
PEER_HEADS = 8
PEER_N_KEYS = 128
PEER_TOPK = 16
RMS_EPS = 1e-6
GN_EPS = HEAD_DIM * 1e-5
NEG_BIG = -1e30
SEQ_BLOCK = 128
FRONT_PAD = SEQ_BLOCK - N_META
VMEM_LIMIT = 48 * 1024 * 1024


def _cparams(*sem):
    return pltpu.CompilerParams(dimension_semantics=sem, vmem_limit_bytes=VMEM_LIMIT)


def _dot(a, b):
    return jnp.dot(a, b, preferred_element_type=F32)


def _dot_nt(a, b):
    return lax.dot_general(a, b, (((1,), (1,)), ((), ())), preferred_element_type=F32)


def _split2(x):
    hi = x.astype(BF16)
    lo = (x - hi.astype(F32)).astype(BF16)
    return hi, lo


def _split3(x):
    hi = x.astype(BF16)
    r = x - hi.astype(F32)
    mid = r.astype(BF16)
    lo = (r - mid.astype(F32)).astype(BF16)
    return hi, mid, lo


def _segsum(x, g):
    hi, lo = _split2(x)
    return _dot(hi, g) + _dot(lo, g)


def _log_sigmoid(z):
    return jnp.minimum(z, 0.0) - jnp.log1p(jnp.exp(-jnp.abs(z)))


def _head_ones(n):
    i = np.arange(n) // HEAD_DIM
    return jnp.asarray(i[:, None] == i[None, :], dtype=BF16)


def _inproj_kernel(x_ref, nw_ref, w_ref, cvec_ref, g_ref, o_ref, xn_ref, *, n_norm, j_fl):
    j = pl.program_id(1)

    @pl.when(j == 0)
    def _():
        x = x_ref[...]
        ms = jnp.mean(x * x, axis=-1, keepdims=True)
        xn_ref[...] = (x * lax.rsqrt(ms + RMS_EPS) * nw_ref[...]).astype(BF16)

    acc = _dot(xn_ref[...], w_ref[...])
    vec = cvec_ref[0]

    @pl.when(j < n_norm)
    def _():
        ss = _segsum(acc * acc, g_ref[...])
        o_ref[...] = acc * lax.rsqrt(ss * (1.0 / HEAD_DIM) + RMS_EPS) * vec

    @pl.when(j == j_fl)
    def _():
        o_ref[...] = _log_sigmoid(acc + vec)

    @pl.when(jnp.logical_and(j >= n_norm, j != j_fl))
    def _():
        o_ref[...] = acc


def _inproj(x, nw, w, cvec, g, *, tm, tn, n_norm, j_fl):
    rows, d = x.shape
    cols = w.shape[1]
    return pl.pallas_call(
        functools.partial(_inproj_kernel, n_norm=n_norm, j_fl=j_fl),
        grid=(rows // tm, cols // tn),
        in_specs=[
            pl.BlockSpec((tm, d), lambda i, j: (i, 0)),
            pl.BlockSpec((1, d), lambda i, j: (0, 0)),
            pl.BlockSpec((d, tn), lambda i, j: (0, j)),
            pl.BlockSpec((1, 1, tn), lambda i, j: (j, 0, 0)),
            pl.BlockSpec((tn, tn), lambda i, j: (0, 0)),
        ],
        out_specs=pl.BlockSpec((tm, tn), lambda i, j: (i, j)),
        out_shape=jax.ShapeDtypeStruct((rows, cols), F32),
        scratch_shapes=[pltpu.VMEM((tm, d), BF16)],
        compiler_params=_cparams("arbitrary", "arbitrary"),
        name="inproj",
    )(x, nw, w, cvec, g)


def _fcum_kernel(lf_ref, ccol_ref, crow_ref, *, nblk):
    r = lax.broadcasted_iota(jnp.int32, (SEQ_BLOCK, SEQ_BLOCK), 0)
    c = lax.broadcasted_iota(jnp.int32, (SEQ_BLOCK, SEQ_BLOCK), 1)
    upper = (c > r).astype(BF16)
    lower = (r > c).astype(BF16)

    def body(n, carry):
        ccar, rcar = carry
        i = nblk - 1 - n
        off = pl.multiple_of(i * SEQ_BLOCK, SEQ_BLOCK)
        x = lf_ref[0, pl.ds(off, SEQ_BLOCK), :]
        h, m, l = _split3(x)
        suf = _dot(upper, h) + _dot(upper, m) + _dot(upper, l)
        ccol_ref[0, pl.ds(off, SEQ_BLOCK), :] = -(suf + ccar)
        xt = x.T
        ht, mt, lt = _split3(xt)
        suft = _dot(ht, lower) + _dot(mt, lower) + _dot(lt, lower)
        crow_ref[0, :, pl.ds(off, SEQ_BLOCK)] = -(suft + rcar)
        return (ccar + suf[0:1, :] + x[0:1, :], rcar + suft[:, 0:1] + xt[:, 0:1])

    lax.fori_loop(0, nblk, body, (jnp.zeros((1, LANES), F32), jnp.zeros((LANES, 1), F32)))


def _fcum(p3, col_block):
    b, lp, _ = p3.shape
    return pl.pallas_call(
        functools.partial(_fcum_kernel, nblk=lp // SEQ_BLOCK),
        grid=(b,),
        in_specs=[pl.BlockSpec((1, lp, LANES), lambda i: (i, 0, col_block))],
        out_specs=[pl.BlockSpec((1, lp, LANES), lambda i: (i, 0, 0)),
                   pl.BlockSpec((1, LANES, lp), lambda i: (i, 0, 0))],
        out_shape=[jax.ShapeDtypeStruct((b, lp, LANES), F32), jax.ShapeDtypeStruct((b, LANES, lp), F32)],
        compiler_params=_cparams("arbitrary"),
        name="fcum",
    )(p3)


def _attn_kernel(q_ref, k_ref, v_ref, ccol_ref, crow_ref, o_ref, *, kc):
    qi = pl.program_id(2)
    tq = SEQ_BLOCK
    lp = k_ref.shape[1]
    lo = lax.broadcasted_iota(jnp.int32, (tq, LANES), 1) < HEAD_DIM
    q = q_ref[0]
    qh = (jnp.where(lo, q, 0.0).astype(BF16), jnp.where(lo, 0.0, q).astype(BF16))
    cq = (ccol_ref[0, 0, :, 0:1], ccol_ref[0, 0, :, 1:2])
    qpos = (qi + 1) * tq + lax.broadcasted_iota(jnp.int32, (tq, kc), 0)
    koff = lax.broadcasted_iota(jnp.int32, (tq, kc), 1)

    def chunk(c, carry):
        first = c * kc
        start = pl.multiple_of(jnp.minimum(first, lp - kc), SEQ_BLOCK)
        kpos = start + koff
        mask = jnp.logical_and(kpos >= jnp.maximum(first, FRONT_PAD), kpos <= qpos)
        kb = k_ref[0, pl.ds(start, kc), :].astype(BF16)
        vb = v_ref[0, pl.ds(start, kc), :].astype(BF16)
        out = []
        for h in range(2):
            m, l, acc = carry[h]
            s = _dot_nt(qh[h], kb) + (cq[h] - crow_ref[0, 0, pl.ds(h, 1), pl.ds(start, kc)])
            s = jnp.where(mask, s, NEG_BIG)
            m_new = jnp.maximum(m, jnp.max(s, axis=-1, keepdims=True))
            p = jnp.exp(s - m_new)
            corr = jnp.exp(m - m_new)
            out.append((m_new, l * corr + jnp.sum(p, axis=-1, keepdims=True),
                        acc * corr + _dot(p.astype(BF16), vb)))
        return tuple(out)

    init = tuple((jnp.full((tq, 1), NEG_BIG, F32), jnp.zeros((tq, 1), F32), jnp.zeros((tq, LANES), F32))
                 for _ in range(2))
    carry = lax.fori_loop(0, ((qi + 2) * tq + kc - 1) // kc, chunk, init)
    o_ref[0] = jnp.where(lo, carry[0][2] / carry[0][1], carry[1][2] / carry[1][1])


def _attention(p3, ccol, crow, *, nq, kcol, vcol):
    b, lp, _ = p3.shape
    nhp = ccol.shape[1]
    kc = max(c for c in (4 * SEQ_BLOCK, 2 * SEQ_BLOCK, SEQ_BLOCK) if c <= lp)
    return pl.pallas_call(
        functools.partial(_attn_kernel, kc=kc),
        grid=(b, nhp, nq),
        in_specs=[
            pl.BlockSpec((1, SEQ_BLOCK, LANES), lambda i, h, q: (i, q + 1, h)),
            pl.BlockSpec((1, lp, LANES), lambda i, h, q: (i, 0, kcol + h)),
            pl.BlockSpec((1, lp, LANES), lambda i, h, q: (i, 0, vcol + h)),
            pl.BlockSpec((1, 1, SEQ_BLOCK, 2), lambda i, h, q: (i, h, q + 1, 0)),
            pl.BlockSpec((1, 1, 2, lp), lambda i, h, q: (i, h, 0, 0)),
        ],
        out_specs=pl.BlockSpec((1, SEQ_BLOCK, LANES), lambda i, h, q: (i, q, h)),
        out_shape=jax.ShapeDtypeStruct((b, nq * SEQ_BLOCK, nhp * LANES), F32),
        compiler_params=_cparams("arbitrary", "arbitrary", "arbitrary"),
        name="attn",
    )(p3, p3, p3, ccol, crow)


def _pagesuf_kernel(lf_ref, suf_ref, tot_ref):
    n = lf_ref.shape[1]
    r = lax.broadcasted_iota(jnp.int32, (n, n), 0)
    c = lax.broadcasted_iota(jnp.int32, (n, n), 1)
    upper = (c > r).astype(BF16)

    def body(i, carry):
        x = lf_ref[i]
        a, b_, c_ = _split3(x)
        suf = _dot(upper, a) + _dot(upper, b_) + _dot(upper, c_)
        suf_ref[i] = suf
        tot_ref[i] = jnp.broadcast_to(suf[0:1, :] + x[0:1, :], x.shape)
        return carry

    lax.fori_loop(0, lf_ref.shape[0], body, 0)


def _page_suffix(lf):
    n, ps, nh = lf.shape
    pb = max(c for c in (32, 16, 8, 4, 2, 1) if n % c == 0)
    spec = pl.BlockSpec((pb, ps, nh), lambda i: (i, 0, 0))
    return pl.pallas_call(
        _pagesuf_kernel,
        grid=(n // pb,),
        in_specs=[spec],
        out_specs=[spec, spec],
        out_shape=[jax.ShapeDtypeStruct(lf.shape, F32)] * 2,
        compiler_params=_cparams("arbitrary"),
        name="pagesuf",
    )(lf)


def _decattn_kernel(pt_ref, q_ref, knew_ref, vnew_ref, bnew_ref, tnew_ref, ck_ref, cv_ref, suf_ref, tot_ref,
                    o_ref, m_ref, l_ref, acc_ref, car_ref, *, n_new, n_heads):
    step = pl.program_id(1)
    tq = SEQ_BLOCK
    tpad = tq // n_heads
    nk = ck_ref.shape[1] * n_heads
    row = lax.broadcasted_iota(jnp.int32, (tq, nk), 0)
    lane = lax.broadcasted_iota(jnp.int32, (tq, nk), 1)
    own = (lane % n_heads) == (row // tpad)

    def process(kpage, vpage, bias, valid, first):
        k2 = kpage.reshape(nk, HEAD_DIM).astype(BF16)
        v2 = vpage.reshape(nk, HEAD_DIM).astype(BF16)
        s = jnp.where(valid, _dot_nt(q_ref[0], k2) + bias, NEG_BIG)
        smax = jnp.max(s, axis=-1, keepdims=True)
        m_new = smax if first else jnp.maximum(m_ref[...], smax)
        p = jnp.exp(s - m_new)
        psum = jnp.sum(p, axis=-1, keepdims=True)
        pv = _dot(p.astype(BF16), v2)
        if first:
            acc_ref[...] = pv
            l_ref[...] = psum
        else:
            corr = jnp.exp(m_ref[...] - m_new)
            acc_ref[...] = acc_ref[...] * corr + pv
            l_ref[...] = l_ref[...] * corr + psum
        m_ref[...] = m_new

    @pl.when(step == 0)
    def _():
        key = lane // n_heads
        valid = jnp.logical_and(own, jnp.logical_and(key <= row % tpad, key < n_new))
        process(knew_ref[0], vnew_ref[0], bnew_ref[0], valid, True)
        car_ref[...] = tnew_ref[0]

    @pl.when(step > 0)
    def _():
        car = car_ref[...]
        process(ck_ref[0], cv_ref[0], suf_ref[0] + car, own, False)
        car_ref[...] = car + tot_ref[0]

    @pl.when(step == pl.num_programs(1) - 1)
    def _():
        o_ref[0] = acc_ref[...] / l_ref[...]


def _decode_attention(page_table, q, knew, vnew, bnew, tnew, ck, cv, suf, tot, *, n_new, n_heads):
    bd, n_pages = page_table.shape
    ps = ck.shape[1]
    nk = ps * n_heads

    def page4(i, s, pt):
        return (pt[i, n_pages - jnp.maximum(s, 1)], 0, 0, 0)

    def page3(i, s, pt):
        return (pt[i, n_pages - jnp.maximum(s, 1)], 0, 0)

    mine4 = lambda i, s, pt: (i, 0, 0, 0)
    mine3 = lambda i, s, pt: (i, 0, 0)
    grid_spec = pltpu.PrefetchScalarGridSpec(
        num_scalar_prefetch=1,
        grid=(bd, n_pages + 1),
        in_specs=[
            pl.BlockSpec((1, SEQ_BLOCK, HEAD_DIM), mine3),
            pl.BlockSpec((1, ps, n_heads, HEAD_DIM), mine4),
            pl.BlockSpec((1, ps, n_heads, HEAD_DIM), mine4),
            pl.BlockSpec((1, 1, nk), mine3),
            pl.BlockSpec((1, 1, nk), mine3),
            pl.BlockSpec((1, ps, n_heads, HEAD_DIM), page4),
            pl.BlockSpec((1, ps, n_heads, HEAD_DIM), page4),
            pl.BlockSpec((1, 1, nk), page3),
            pl.BlockSpec((1, 1, nk), page3),
        ],
        out_specs=pl.BlockSpec((1, SEQ_BLOCK, HEAD_DIM), mine3),
        scratch_shapes=[pltpu.VMEM((SEQ_BLOCK, 1), F32), pltpu.VMEM((SEQ_BLOCK, 1), F32),
                        pltpu.VMEM((SEQ_BLOCK, HEAD_DIM), F32), pltpu.VMEM((1, nk), F32)],
    )
    return pl.pallas_call(
        functools.partial(_decattn_kernel, n_new=n_new, n_heads=n_heads),
        grid_spec=grid_spec,
        out_shape=jax.ShapeDtypeStruct((bd, SEQ_BLOCK, HEAD_DIM), F32),
        compiler_params=_cparams("arbitrary", "arbitrary"),
        name="decattn",
    )(page_table, q, knew, vnew, bnew, tnew, ck, cv, suf, tot)


def _rwprep_kernel(*refs, seq_len, has_start):
    n_in = 4 + (4 if has_start else 0)
    p_refs = refs[0:4]
    st_refs = refs[4:8] if has_start else None
    (mur, muk, muv, mul, w0, w2, a0, a2, g2, kkw, kaw, rkw, g_ref) = refs[n_in:n_in + 13]
    (r_o, w_o, k_o, v_o, nkk_o, kka_o, gate_o, bon_o) = refs[n_in + 13:n_in + 21]
    carry = refs[n_in + 21:n_in + 25]
    i = pl.program_id(0)
    tm = p_refs[0].shape[0]

    @pl.when(i == 0)
    def _():
        for cr in carry:
            cr[...] = jnp.zeros_like(cr)

    mixed = []
    for n, (pr, mu) in enumerate(zip(p_refs, (mur, muk, muv, mul))):
        p = pr[...]
        row = lax.broadcasted_iota(jnp.int32, p.shape, 0)
        prev = jnp.where(row == 0, carry[n][0:1, :], pltpu.roll(p, 1, axis=0))
        if has_start:
            prev = jnp.where(row % seq_len == 0, st_refs[n][...], prev)
        carry[n][0:1, :] = p[tm - 1:tm, :]
        mixed.append(p + mu[...] * (prev - p))
    r, k, v, lora = mixed
    gm = g_ref[...]
    wa = lora[:, 0:LANES]
    wlog = _log_sigmoid(w0[...] + _dot(jnp.tanh(wa).astype(BF16), w2[...])) - 0.5
    decay = jnp.exp(-jnp.exp(wlog))
    a = jax.nn.sigmoid(a0[...] + _dot(wa.astype(BF16), a2[...]))
    gate = _dot(jax.nn.sigmoid(lora[:, LANES:]).astype(BF16), g2[...])
    kk = k * kkw[...]
    kk = kk * lax.rsqrt(_segsum(kk * kk, gm) + 1e-12)
    k2 = k * (1.0 + (a - 1.0) * kaw[...])
    r_o[...] = r
    w_o[...] = decay
    k_o[...] = k2
    v_o[...] = v
    nkk_o[...] = -kk
    kka_o[...] = kk * a
    gate_o[...] = gate
    bon_o[...] = _segsum(r * k2 * rkw[...], gm) * v


def _rwprep(p2, starts, params, gm, *, tm, seq_len, rcol, lcol):
    rows = p2.shape[0]
    wid = gm.shape[0]
    lw = 2 * LANES
    has_start = starts is not None
    row_spec = lambda w, cb: pl.BlockSpec((tm, w), lambda i: (i, cb))
    in_specs = [row_spec(wid, rcol), row_spec(wid, rcol + 1), row_spec(wid, rcol + 2), row_spec(lw, lcol)]
    args = [p2, p2, p2, p2]
    if has_start:
        in_specs += [row_spec(wid, 0), row_spec(wid, 0), row_spec(wid, 0), row_spec(lw, 0)]
        args += list(starts)
    for prm in params:
        in_specs.append(pl.BlockSpec(prm.shape, lambda i: (0, 0)))
    in_specs.append(pl.BlockSpec(gm.shape, lambda i: (0, 0)))
    args += list(params) + [gm]
    return pl.pallas_call(
        functools.partial(_rwprep_kernel, seq_len=seq_len, has_start=has_start),
        grid=(rows // tm,),
        in_specs=in_specs,
        out_specs=[pl.BlockSpec((tm, wid), lambda i: (i, 0))] * 8,
        out_shape=[jax.ShapeDtypeStruct((rows, wid), F32)] * 8,
        scratch_shapes=[pltpu.VMEM((8, wid), F32)] * 3 + [pltpu.VMEM((8, lw), F32)],
        compiler_params=_cparams("arbitrary"),
        name="rwprep",
    )(*args)


def _scan_kernel(r_ref, w_ref, k_ref, v_ref, a_ref, b_ref, s0_ref, y_ref, sT_ref, s_ref, *, tb):
    tblk = pl.program_id(1)
    dk = s_ref.shape[0]

    @pl.when(tblk == 0)
    def _():
        s_ref[...] = s0_ref[...]

    def step(t, carry):
        sa = s_ref[0] * a_ref[t, pl.ds(0, 1), :]
        for kx in range(1, dk):
            sa = sa + s_ref[kx] * a_ref[t, pl.ds(kx, 1), :]
        vt = v_ref[t]
        y = None
        for kx in range(dk):
            sk = (s_ref[kx] * w_ref[t, pl.ds(kx, 1), :] + sa * b_ref[t, pl.ds(kx, 1), :]
                  + vt * k_ref[t, pl.ds(kx, 1), :])
            s_ref[kx] = sk
            yk = sk * r_ref[t, pl.ds(kx, 1), :]
            y = yk if y is None else y + yk
        y_ref[t] = y
        return carry

    lax.fori_loop(0, tb, step, 0)

    @pl.when(tblk == pl.num_programs(1) - 1)
    def _():
        sT_ref[...] = s_ref[...]


def _scan(seqs, s0, *, tb, t_off, n_steps):
    npairs = s0.shape[-1]
    dk, dv = s0.shape[0], s0.shape[1]
    seq_spec = pl.BlockSpec((tb, dk, LANES), lambda g, t: (t + t_off // tb, 0, g))
    st_spec = pl.BlockSpec((dk, dv, LANES), lambda g, t: (0, 0, g))
    return pl.pallas_call(
        functools.partial(_scan_kernel, tb=tb),
        grid=(npairs // LANES, n_steps // tb),
        in_specs=[seq_spec] * 6 + [st_spec],
        out_specs=[pl.BlockSpec((tb, dv, LANES), lambda g, t: (t, 0, g)), st_spec],
        out_shape=[jax.ShapeDtypeStruct((n_steps, dv, npairs), F32), jax.ShapeDtypeStruct(s0.shape, F32)],
        scratch_shapes=[pltpu.VMEM((dk, dv, LANES), F32)],
        compiler_params=_cparams("arbitrary", "arbitrary"),
        name="wkvscan",
    )(*seqs, s0)


def _mix_kernel(x_ref, o_ref, og_ref, y_ref, gate_ref, bon_ref, on_ref, lnw_ref, lnb_ref, n2_ref,
                wo_ref, g_ref, h_ref, hn_ref):
    gm = g_ref[...]
    half = o_ref.shape[1]
    o = o_ref[...]
    fox = (o * lax.rsqrt(_segsum(o * o, gm) * (1.0 / HEAD_DIM) + RMS_EPS) * on_ref[...]
           * jax.nn.sigmoid(og_ref[...]))
    y = y_ref[...]
    d = y - _segsum(y, gm) * (1.0 / HEAD_DIM)
    var = _segsum(d * d, gm) * (1.0 / HEAD_DIM)
    rw = (d * lax.rsqrt(var + GN_EPS) * lnw_ref[...] + lnb_ref[...] + bon_ref[...]) * gate_ref[...]
    h = (x_ref[...] + _dot(fox.astype(BF16), wo_ref[0:half, :]) + _dot(rw.astype(BF16), wo_ref[half:, :]))
    h_ref[...] = h
    ms = jnp.mean(h * h, axis=-1, keepdims=True)
    hn_ref[...] = h * lax.rsqrt(ms + RMS_EPS) * n2_ref[...]


def _mix(x, o, og_src, y, gate_src, bon_src, vecs, wo, gm, *, n_blocks, og_map, rw_map):
    tm = SEQ_BLOCK
    d = x.shape[1]
    half = d // 2
    blk = lambda w, f: pl.BlockSpec((tm, w), f)
    const = lambda a: pl.BlockSpec(a.shape, lambda i: (0, 0))
    return pl.pallas_call(
        _mix_kernel,
        grid=(n_blocks,),
        in_specs=[blk(d, lambda i: (i, 0)), blk(half, lambda i: (i, 0)), blk(half, og_map),
                  blk(half, lambda i: (i, 0)), blk(half, rw_map), blk(half, rw_map)]
        + [const(v) for v in vecs] + [const(wo), const(gm)],
        out_specs=[blk(d, lambda i: (i, 0))] * 2,
        out_shape=[jax.ShapeDtypeStruct((n_blocks * tm, d), F32)] * 2,
        compiler_params=_cparams("arbitrary"),
        name="mix",
    )(x, o, og_src, y, gate_src, bon_src, *vecs, wo, gm)


def _topk_rows(s_ref, n_rows, out_v, out_i, idx_ref=None):
    pos = lax.broadcasted_iota(jnp.int32, (n_rows, s_ref.shape[1]), 0)
    for t in range(PEER_TOPK):
        s = s_ref[...]
        m = jnp.max(s, axis=0, keepdims=True)
        am = jnp.min(jnp.where(s == m, pos, n_rows), axis=0, keepdims=True)
        hit = pos == am
        out_v[pl.ds(t, 1), :] = m
        if idx_ref is None:
            out_i[pl.ds(t, 1), :] = am
        else:
            out_i[pl.ds(t, 1), :] = jnp.max(jnp.where(hit, idx_ref[...], -1), axis=0, keepdims=True)
        s_ref[...] = jnp.where(hit, -jnp.inf, s)


def _peersel_kernel(hn_ref, wq_ref, keys_ref, eidx_ref, gate_ref, s_ref, sv_ref, si_ref, cand_ref, cidx_ref,
                    fv_ref, fi_ref):
    q = _dot(hn_ref[...].astype(BF16), wq_ref[...])
    for h in range(PEER_HEADS):
        qh = q[:, h * LANES:(h + 1) * LANES].astype(BF16)
        for half in range(2):
            s_ref[...] = _dot_nt(keys_ref[h, half], qh)
            _topk_rows(s_ref, PEER_N_KEYS, sv_ref.at[half], si_ref.at[half])
        sv1 = sv_ref[1]
        si1 = si_ref[1]
        for i in range(PEER_TOPK):
            rows = slice(i * PEER_TOPK, (i + 1) * PEER_TOPK)
            cand_ref[rows, :] = sv_ref[0, pl.ds(i, 1), :] + sv1
            cidx_ref[rows, :] = si_ref[0, pl.ds(i, 1), :] * PEER_N_KEYS + si1
        _topk_rows(cand_ref, PEER_TOPK * PEER_TOPK, fv_ref, fi_ref, idx_ref=cidx_ref)
        fv = fv_ref[...]
        e = jnp.exp(fv - fv[0:1, :])
        rows = slice(h * PEER_TOPK, (h + 1) * PEER_TOPK)
        gate_ref[rows, :] = e / jnp.sum(e, axis=0, keepdims=True)
        eidx_ref[rows, :] = fi_ref[...]


def _peer_select(hn, wq, keys):
    n, d = hn.shape
    tm = SEQ_BLOCK
    nsel = PEER_HEADS * PEER_TOPK
    ncand = PEER_TOPK * PEER_TOPK
    return pl.pallas_call(
        _peersel_kernel,
        grid=(n // tm,),
        in_specs=[pl.BlockSpec((tm, d), lambda i: (i, 0)),
                  pl.BlockSpec(wq.shape, lambda i: (0, 0)),
                  pl.BlockSpec(keys.shape, lambda i: (0, 0, 0, 0))],
        out_specs=[pl.BlockSpec((nsel, tm), lambda i: (0, i))] * 2,
        out_shape=[jax.ShapeDtypeStruct((nsel, n), jnp.int32), jax.ShapeDtypeStruct((nsel, n), F32)],
        scratch_shapes=[pltpu.VMEM((PEER_N_KEYS, tm), F32),
                        pltpu.VMEM((2, PEER_TOPK, tm), F32), pltpu.VMEM((2, PEER_TOPK, tm), jnp.int32),
                        pltpu.VMEM((ncand, tm), F32), pltpu.VMEM((ncand, tm), jnp.int32),
                        pltpu.VMEM((PEER_TOPK, tm), F32), pltpu.VMEM((PEER_TOPK, tm), jnp.int32)],
        compiler_params=_cparams("arbitrary"),
        name="peersel",
    )(hn, wq, keys)


def _peer_kernel(eidx_hbm, table_hbm, x_ref, gate_ref, hres_ref, out_ref, idx_smem, buf, part_ref, sem_idx, sems,
                 *, tb, nsel, rows_per_expert, nslot):
    i = pl.program_id(0)
    rpe = rows_per_expert
    nchunk = rpe // 2
    slot_rows = nsel * rpe
    cp = pltpu.make_async_copy(eidx_hbm.at[pl.ds(pl.multiple_of(i * (tb * nsel), tb * nsel), tb * nsel)],
                               idx_smem, sem_idx)
    cp.start()
    cp.wait()

    grp = 8
    ngrp = nsel // grp

    def issue(t, slot):
        def body(g, carry):
            for j in range(grp):
                kx = g * grp + j
                e = idx_smem[t * nsel + kx]
                pltpu.make_async_copy(table_hbm.at[pl.ds(pl.multiple_of(e * rpe, rpe), rpe), :],
                                      buf.at[slot, pl.ds(pl.multiple_of(kx * rpe, rpe), rpe), :],
                                      sems.at[slot]).start()
            return carry

        lax.fori_loop(0, ngrp, body, 0)

    def wait(slot):
        pltpu.make_async_copy(table_hbm.at[pl.ds(0, slot_rows), :], buf.at[slot], sems.at[slot]).wait()

    lane = lax.broadcasted_iota(jnp.int32, gate_ref.shape, 1)
    sub8 = lax.broadcasted_iota(jnp.int32, (8, LANES), 0)

    def fold(a, b, shift):
        low = (sub8 & shift) == 0
        return jnp.where(low, a + pltpu.roll(a, 8 - shift, axis=0), b + pltpu.roll(b, shift, axis=0))

    def sublane_sums(ts):
        ys = [fold(ts[j], ts[j + 4], 4) for j in range(4)]
        zs = [fold(ys[j], ys[j + 2], 2) for j in range(2)]
        return fold(zs[0], zs[1], 1)

    def compute(t, slot):
        xt = x_ref[t]

        def hidden(g, carry):
            base = pl.multiple_of(g * (grp * rpe), grp * rpe)
            ts = []
            for j in range(grp):
                p = buf[slot, pl.ds(base + j * rpe, nchunk), :] * xt
                acc = p[0:8]
                for s in range(8, nchunk, 8):
                    acc = acc + p[s:s + 8]
                ts.append(acc)
            part_ref[pl.ds(pl.multiple_of(g * grp, grp), grp), :] = sublane_sums(ts)
            return carry

        lax.fori_loop(0, ngrp, hidden, 0)
        hid = jnp.sum(part_ref[...], axis=-1, keepdims=True)
        gate = jnp.sum(jnp.where(lane == t, gate_ref[...], 0.0), axis=-1, keepdims=True)
        act = gate * (hid * (lax.erf(hid / np.sqrt(2.0).astype(np.float32)) + 1.0) / 2.0)
        part_ref[...] = jnp.broadcast_to(act, (nsel, LANES))

        def combine(g, accs):
            base = pl.multiple_of(g * (grp * rpe), grp * rpe)
            a8 = part_ref[pl.ds(pl.multiple_of(g * grp, grp), grp), :]
            accs = list(accs)
            for j in range(grp):
                accs[j % 4] = accs[j % 4] + buf[slot, pl.ds(base + j * rpe + nchunk, nchunk), :] * a8[j:j + 1, :]
            return tuple(accs)

        zero = jnp.zeros((nchunk, LANES), F32)
        accs = lax.fori_loop(0, ngrp, combine, (zero, zero, zero, zero))
        out_ref[t] = hres_ref[t] + ((accs[0] + accs[1]) + (accs[2] + accs[3]))

    for t in range(nslot - 1):
        issue(t, t)

    def body(t, carry):
        @pl.when(t + nslot - 1 < tb)
        def _():
            issue(t + nslot - 1, (t + nslot - 1) % nslot)

        slot = t % nslot
        wait(slot)
        compute(t, slot)
        return carry

    lax.fori_loop(0, tb, body, 0)


def _peer_combine(eidx_flat, table, hn, gate_t, hres, *, nsel, rows_per_expert):
    n, d = hn.shape
    tb = SEQ_BLOCK
    nslot = 4
    nchunk = d // LANES
    tok = pl.BlockSpec((tb, nchunk, LANES), lambda i: (i, 0, 0))
    out = pl.pallas_call(
        functools.partial(_peer_kernel, tb=tb, nsel=nsel, rows_per_expert=rows_per_expert, nslot=nslot),
        grid=(n // tb,),
        in_specs=[pl.BlockSpec(memory_space=pl.ANY),
                  pl.BlockSpec(memory_space=pl.ANY),
                  tok,
                  pl.BlockSpec((nsel, tb), lambda i: (0, i)),
                  tok],
        out_specs=tok,
        out_shape=jax.ShapeDtypeStruct((n, nchunk, LANES), F32),
        scratch_shapes=[pltpu.SMEM((tb * nsel,), jnp.int32),
                        pltpu.VMEM((nslot, nsel * rows_per_expert, LANES), F32),
                        pltpu.VMEM((nsel, LANES), F32),
                        pltpu.SemaphoreType.DMA(()),
                        pltpu.SemaphoreType.DMA((nslot,))],
        compiler_params=_cparams("arbitrary"),
        name="peer",
    )(eidx_flat, table, hn.reshape(n, nchunk, LANES), gate_t, hres.reshape(n, nchunk, LANES))
    return out.reshape(n, d)


def _to_pairs(a, b, l):
    nh = a.shape[1] // HEAD_DIM
    a = a.reshape(b, l, nh, HEAD_DIM).transpose(1, 3, 0, 2).reshape(l, HEAD_DIM, b * nh)
    pad = (-a.shape[-1]) % LANES
    return jnp.pad(a, ((0, 0), (0, 0), (0, pad))) if pad else a


def kernel(x_prompt, x_sample, cache_k, cache_v, cache_logf, state_wkv, state_shift, page_table, meta_tokens,
           norm1_w, w_in, fox_bf, fox_qn, fox_kn, fox_on, rw_mu, rw_w0, rw_w2, rw_a0, rw_a2, rw_g2, rw_kk,
           rw_ka, rw_rk, rw_ln_w, rw_ln_b, w_o, norm2_w, peer_wq, peer_keys, peer_u, peer_v):
    depth = w_in.shape[0]
    assert depth == 1, "single-layer step"
    bsz, seq, d = x_prompt.shape
    bd, tdec, _ = x_sample.shape
    half = d // 2
    nh = half // HEAD_DIM
    nhp = half // LANES
    lp = FRONT_PAD + N_META + seq
    nq = seq // SEQ_BLOCK
    lora_w = 2 * LANES
    assert seq % SEQ_BLOCK == 0 and rw_w2.shape[1] + rw_a2.shape[1] == LANES and rw_g2.shape[1] == LANES

    tn = lora_w
    w = w_in[0]
    fox_cols = 4 * half + nh
    wq_, wk_, wv_ = w[:, 0:half], w[:, half:2 * half], w[:, 2 * half:3 * half]
    wfl, wog = w[:, 3 * half:3 * half + nh], w[:, 3 * half + nh:fox_cols]
    wrw = w[:, fox_cols:]
    w_all = jnp.concatenate([wq_, wk_, wv_, wog, wrw, jnp.pad(wfl, ((0, 0), (0, tn - nh)))], axis=1).astype(BF16)
    ncol = w_all.shape[1]
    nblk = ncol // tn
    n_norm = 2 * half // tn
    j_fl = nblk - 1
    rep = tn // HEAD_DIM
    cvec = jnp.zeros((nblk, tn), F32)
    cvec = cvec.at[0:n_norm // 2].set(jnp.tile(fox_qn[0] * (HEAD_DIM ** -0.5), rep))
    cvec = cvec.at[n_norm // 2:n_norm].set(jnp.tile(fox_kn[0], rep))
    cvec = cvec.at[j_fl, 0:nh].set(fox_bf[0]).reshape(nblk, 1, tn)
    g_tn = _head_ones(tn)
    g_half = _head_ones(half)
    nw1 = norm1_w[0].reshape(1, d)
    col_k, col_v, col_og, col_rw = half, 2 * half, 3 * half, 4 * half
    col_lora = col_rw + 3 * half
    col_fl = col_lora + lora_w

    mu = rw_mu[0]
    row = lambda a: a.reshape(1, -1)
    zpad = jnp.zeros((LANES - rw_w2.shape[1], half), F32)
    rw_params = [row(mu[0:half]), row(mu[half:2 * half]), row(mu[2 * half:3 * half]), row(mu[3 * half:]),
                 row(rw_w0[0]), jnp.concatenate([rw_w2[0], zpad], 0).astype(BF16),
                 row(rw_a0[0]), jnp.concatenate([zpad, rw_a2[0]], 0).astype(BF16),
                 rw_g2[0].astype(BF16), row(rw_kk[0]), row(rw_ka[0]), row(rw_rk[0])]
    mix_vecs = [row(fox_on[0]), row(rw_ln_w[0]), row(rw_ln_b[0]), row(norm2_w[0])]
    wo = w_o[0].astype(BF16)
    wq_peer = peer_wq[0].astype(BF16)
    khalf = peer_keys.shape[-1]
    keys = peer_keys[0].astype(BF16)
    keys = jnp.stack([jnp.pad(keys[:, 0], ((0, 0), (0, 0), (0, LANES - khalf))),
                      jnp.pad(keys[:, 1], ((0, 0), (0, 0), (khalf, LANES - 2 * khalf))) ], axis=1)
    rpe = 2 * d // LANES
    table = jnp.concatenate([peer_u[0], peer_v[0]], axis=1).reshape(-1, LANES)
    nsel = PEER_HEADS * PEER_TOPK

    def peer(hn, hres):
        eidx_t, gate_t = _peer_select(hn, wq_peer, keys)
        return _peer_combine(eidx_t.T.reshape(-1), table, hn, gate_t, hres, nsel=nsel, rows_per_expert=rpe)

    hp = jnp.concatenate([jnp.zeros((bsz, FRONT_PAD, d), F32),
                          jnp.broadcast_to(meta_tokens.astype(F32)[None], (bsz, N_META, d)),
                          x_prompt.astype(F32)], axis=1).reshape(bsz * lp, d)
    tm_p = 1024 if (bsz * lp) % 1024 == 0 else SEQ_BLOCK
    pp = _inproj(hp, nw1, w_all, cvec, g_tn, tm=tm_p, tn=tn, n_norm=n_norm, j_fl=j_fl)
    pp3 = pp.reshape(bsz, lp, ncol)
    ccol, crow = _fcum(pp3, col_fl // LANES)
    ccol = ccol[:, :, 0:nh].reshape(bsz, lp, nhp, 2).transpose(0, 2, 1, 3)
    crow = crow[:, 0:nh, :].reshape(bsz, nhp, 2, lp)
    o_p = _attention(pp3, ccol, crow, nq=nq, kcol=col_k // LANES, vcol=col_v // LANES)
    o_p = o_p.reshape(bsz * seq, half)

    rw_p = _rwprep(pp, None, rw_params, g_half, tm=SEQ_BLOCK, seq_len=lp,
                   rcol=col_rw // half, lcol=col_lora // lora_w)
    seqs_p = [_to_pairs(a, bsz, lp) for a in rw_p[0:6]]
    npair_p = seqs_p[0].shape[-1]
    tb_p = N_META
    y_p, s_p = _scan(seqs_p, jnp.zeros((HEAD_DIM, HEAD_DIM, npair_p), F32),
                     tb=tb_p, t_off=FRONT_PAD, n_steps=N_META + seq)
    y_p = y_p[N_META:, :, 0:bsz * nh].reshape(seq, HEAD_DIM, bsz, nh).transpose(2, 0, 3, 1).reshape(bsz * seq, half)
    wkv_p = s_p[:, :, 0:bsz * nh].transpose(2, 1, 0).reshape(1, bsz, nh, HEAD_DIM, HEAD_DIM)

    blocks_per_seq = lp // SEQ_BLOCK
    in_rows = lambda i: (i // nq) * blocks_per_seq + 1 + i % nq
    h_p, hn_p = _mix(x_prompt.reshape(bsz * seq, d).astype(F32), o_p, pp, y_p, rw_p[6], rw_p[7], mix_vecs, wo,
                     g_half, n_blocks=bsz * nq,
                     og_map=lambda i: (in_rows(i), col_og // half), rw_map=lambda i: (in_rows(i), 0))
    y_prompt = peer(hn_p, h_p).reshape(bsz, seq, d)

    pr = pp3[:, FRONT_PAD:]
    k_prompt = pr[:, :, col_k:col_k + half].reshape(1, bsz, N_META + seq, nh, HEAD_DIM)
    v_prompt = pr[:, :, col_v:col_v + half].reshape(1, bsz, N_META + seq, nh, HEAD_DIM)
    logf_prompt = pr[:, :, col_fl:col_fl + nh].reshape(1, bsz, N_META + seq, nh)
    shift_prompt = pp3[:, -1, col_rw:col_lora + lora_w].reshape(1, bsz, -1)

    nrow_d = bd * tdec
    rows_d = -(-nrow_d // SEQ_BLOCK) * SEQ_BLOCK
    xs = x_sample.astype(F32).reshape(nrow_d, d)
    xs_pad = jnp.pad(xs, ((0, rows_d - nrow_d), (0, 0)))
    pd_ = _inproj(xs_pad, nw1, w_all, cvec, g_tn, tm=SEQ_BLOCK, tn=tn, n_norm=n_norm, j_fl=j_fl)
    pd3 = pd_[0:nrow_d].reshape(bd, tdec, ncol)
    tpad = SEQ_BLOCK // nh
    n_pool, page_size = cache_k.shape[1], cache_k.shape[2]
    nk = page_size * nh
    q4 = pd3[:, :, 0:half].reshape(bd, tdec, nh, HEAD_DIM).transpose(0, 2, 1, 3)
    q_d = jnp.pad(q4, ((0, 0), (0, 0), (0, tpad - tdec), (0, 0))).reshape(bd, SEQ_BLOCK, HEAD_DIM).astype(BF16)
    padk = ((0, 0), (0, page_size - tdec), (0, 0), (0, 0))
    knew = jnp.pad(pd3[:, :, col_k:col_k + half].reshape(bd, tdec, nh, HEAD_DIM), padk)
    vnew = jnp.pad(pd3[:, :, col_v:col_v + half].reshape(bd, tdec, nh, HEAD_DIM), padk)
    lnew = jnp.pad(pd3[:, :, col_fl:col_fl + nh], ((0, 0), (0, page_size - tdec), (0, 0)))
    bnew, tnew = _page_suffix(lnew)
    suf, tot = _page_suffix(cache_logf[0].astype(F32))
    o_d = _decode_attention(page_table, q_d, knew, vnew, bnew.reshape(bd, 1, nk), tnew.reshape(bd, 1, nk),
                            cache_k[0].astype(F32), cache_v[0].astype(F32),
                            suf.reshape(n_pool, 1, nk), tot.reshape(n_pool, 1, nk), n_new=tdec, n_heads=nh)
    o_d = o_d.reshape(bd, nh, tpad, HEAD_DIM)[:, :, 0:tdec].transpose(0, 2, 1, 3).reshape(nrow_d, half)
    o_d = jnp.pad(o_d, ((0, rows_d - nrow_d), (0, 0)))

    sh = state_shift[0].astype(F32)
    st = jnp.pad(jnp.repeat(sh, tdec, axis=0), ((0, rows_d - nrow_d), (0, 0)))
    starts = [st[:, 0:half], st[:, half:2 * half], st[:, 2 * half:3 * half], st[:, 3 * half:]]
    rw_d = _rwprep(pd_, starts, rw_params, g_half, tm=SEQ_BLOCK, seq_len=tdec,
                   rcol=col_rw // half, lcol=col_lora // lora_w)
    seqs_d = [_to_pairs(a[0:nrow_d], bd, tdec) for a in rw_d[0:6]]
    s0_d = state_wkv[0].astype(F32).reshape(bd * nh, HEAD_DIM, HEAD_DIM).transpose(2, 1, 0)
    pad_d = (-s0_d.shape[-1]) % LANES
    if pad_d:
        s0_d = jnp.pad(s0_d, ((0, 0), (0, 0), (0, pad_d)))
    y_d, s_d = _scan(seqs_d, s0_d, tb=tdec, t_off=0, n_steps=tdec)
    y_d = y_d[:, :, 0:bd * nh].reshape(tdec, HEAD_DIM, bd, nh).transpose(2, 0, 3, 1).reshape(nrow_d, half)
    y_d = jnp.pad(y_d, ((0, rows_d - nrow_d), (0, 0)))
    wkv_d = s_d[:, :, 0:bd * nh].transpose(2, 1, 0).reshape(1, bd, nh, HEAD_DIM, HEAD_DIM)
    h_d, hn_d = _mix(xs_pad, o_d, pd_, y_d, rw_d[6], rw_d[7], mix_vecs, wo, g_half, n_blocks=rows_d // SEQ_BLOCK,
                     og_map=lambda i: (i, col_og // half), rw_map=lambda i: (i, 0))
    y_sample = peer(hn_d, h_d)[0:nrow_d].reshape(bd, tdec, d)

    k_sample = pd3[:, :, col_k:col_k + half].reshape(1, bd, tdec, nh, HEAD_DIM)
    v_sample = pd3[:, :, col_v:col_v + half].reshape(1, bd, tdec, nh, HEAD_DIM)
    logf_sample = pd3[:, :, col_fl:col_fl + nh].reshape(1, bd, tdec, nh)
    shift_sample = pd3[:, -1, col_rw:col_lora + lora_w].reshape(1, bd, -1)

    return (y_prompt, y_sample, k_prompt, v_prompt, logf_prompt, wkv_p, shift_prompt,
            k_sample, v_sample, logf_sample, wkv_d, shift_sample)
```

```python
import functools

import numpy as np
import jax
import jax.numpy as jnp
from jax import lax
from jax.experimental import pallas as pl
from jax.experimental.pallas import tpu as pltpu

F32 = jnp.float32
BF16 = jnp.bfloat16

LANES = 128
HEAD_DIM = 64
N_META = 16
PEER_HEADS = 8
PEER_N_KEYS = 128
PEER_TOPK = 16
RMS_EPS = 1e-6
GN_EPS = HEAD_DIM * 1e-5
NEG_BIG = -1e30
SEQ_BLOCK = 128
FRONT_PAD = SEQ_BLOCK - N_META
VMEM_LIMIT = 48 * 1024 * 1024


def _cparams(*sem):
    return pltpu.CompilerParams(dimension_semantics=sem, vmem_limit_bytes=VMEM_LIMIT)


def _dot(a, b):
    return jnp.dot(a, b, preferred_element_type=F32)


def _dot_nt(a, b):
    return lax.dot_general(a, b, (((1,), (1,)), ((), ())), preferred_element_type=F32)


def _split2(x):
    hi = x.astype(BF16)
    lo = (x - hi.astype(F32)).astype(BF16)
    return hi, lo


def _split3(x):
    hi = x.astype(BF16)
    r = x - hi.astype(F32)
    mid = r.astype(BF16)
    lo = (r - mid.astype(F32)).astype(BF16)
    return hi, mid, lo


def _segsum(x, g):
    hi, lo = _split2(x)
    return _dot(hi, g) + _dot(lo, g)


def _log_sigmoid(z):
    return jnp.minimum(z, 0.0) - jnp.log1p(jnp.exp(-jnp.abs(z)))


def _head_ones(n):
    i = np.arange(n) // HEAD_DIM
    return jnp.asarray(i[:, None] == i[None, :], dtype=BF16)


def _inproj_kernel(x_ref, nw_ref, w_ref, cvec_ref, g_ref, o_ref, xn_ref, *, n_norm, j_fl):
    j = pl.program_id(1)

    @pl.when(j == 0)
    def _():
        x = x_ref[...]
        ms = jnp.mean(x * x, axis=-1, keepdims=True)
        xn_ref[...] = (x * lax.rsqrt(ms + RMS_EPS) * nw_ref[...]).astype(BF16)

    acc = _dot(xn_ref[...], w_ref[...])
    vec = cvec_ref[0]

    @pl.when(j < n_norm)
    def _():
        ss = _segsum(acc * acc, g_ref[...])
        o_ref[...] = acc * lax.rsqrt(ss * (1.0 / HEAD_DIM) + RMS_EPS) * vec

    @pl.when(j == j_fl)
    def _():
        o_ref[...] = _log_sigmoid(acc + vec)

    @pl.when(jnp.logical_and(j >= n_norm, j != j_fl))
    def _():
        o_ref[...] = acc


def _inproj(x, nw, w, cvec, g, *, tm, tn, n_norm, j_fl):
    rows, d = x.shape
    cols = w.shape[1]
    return pl.pallas_call(
        functools.partial(_inproj_kernel, n_norm=n_norm, j_fl=j_fl),
        grid=(rows // tm, cols // tn),
        in_specs=[
            pl.BlockSpec((tm, d), lambda i, j: (i, 0)),
            pl.BlockSpec((1, d), lambda i, j: (0, 0)),
            pl.BlockSpec((d, tn), lambda i, j: (0, j)),
            pl.BlockSpec((1, 1, tn), lambda i, j: (j, 0, 0)),
            pl.BlockSpec((tn, tn), lambda i, j: (0, 0)),
        ],
        out_specs=pl.BlockSpec((tm, tn), lambda i, j: (i, j)),
        out_shape=jax.ShapeDtypeStruct((rows, cols), F32),
        scratch_shapes=[pltpu.VMEM((tm, d), BF16)],
        compiler_params=_cparams("arbitrary", "arbitrary"),
        name="inproj",
    )(x, nw, w, cvec, g)


def _fcum_kernel(lf_ref, ccol_ref, crow_ref, *, nblk):
    r = lax.broadcasted_iota(jnp.int32, (SEQ_BLOCK, SEQ_BLOCK), 0)
    c = lax.broadcasted_iota(jnp.int32, (SEQ_BLOCK, SEQ_BLOCK), 1)
    upper = (c > r).astype(BF16)
    lower = (r > c).astype(BF16)

    def body(n, carry):
        ccar, rcar = carry
        i = nblk - 1 - n
        off = pl.multiple_of(i * SEQ_BLOCK, SEQ_BLOCK)
        x = lf_ref[0, pl.ds(off, SEQ_BLOCK), :]
        h, m, l = _split3(x)
        suf = _dot(upper, h) + _dot(upper, m) + _dot(upper, l)
        ccol_ref[0, pl.ds(off, SEQ_BLOCK), :] = -(suf + ccar)
        xt = x.T
        ht, mt, lt = _split3(xt)
        suft = _dot(ht, lower) + _dot(mt, lower) + _dot(lt, lower)
        crow_ref[0, :, pl.ds(off, SEQ_BLOCK)] = -(suft + rcar)
        return (ccar + suf[0:1, :] + x[0:1, :], rcar + suft[:, 0:1] + xt[:, 0:1])

    lax.fori_loop(0, nblk, body, (jnp.zeros((1, LANES), F32), jnp.zeros((LANES, 1), F32)))


def _fcum(p3, col_block):
    b, lp, _ = p3.shape
    return pl.pallas_call(
        functools.partial(_fcum_kernel, nblk=lp // SEQ_BLOCK),
        grid=(b,),
        in_specs=[pl.BlockSpec((1, lp, LANES), lambda i: (i, 0, col_block))],
        out_specs=[pl.BlockSpec((1, lp, LANES), lambda i: (i, 0, 0)),
                   pl.BlockSpec((1, LANES, lp), lambda i: (i, 0, 0))],
        out_shape=[jax.ShapeDtypeStruct((b, lp, LANES), F32), jax.ShapeDtypeStruct((b, LANES, lp), F32)],
        compiler_params=_cparams("arbitrary"),
        name="fcum",
    )(p3)


def _attn_kernel(q_ref, k_ref, v_ref, ccol_ref, crow_ref, o_ref, *, kc):
    qi = pl.program_id(2)
    tq = SEQ_BLOCK
    lp = k_ref.shape[1]
    lo = lax.broadcasted_iota(jnp.int32, (tq, LANES), 1) < HEAD_DIM
    q = q_ref[0]
    qh = (jnp.where(lo, q, 0.0).astype(BF16), jnp.where(lo, 0.0, q).astype(BF16))
    cq = (ccol_ref[0, 0, :, 0:1], ccol_ref[0, 0, :, 1:2])
    qpos = (qi + 1) * tq + lax.broadcasted_iota(jnp.int32, (tq, kc), 0)
    koff = lax.broadcasted_iota(jnp.int32, (tq, kc), 1)

    def chunk(c, carry):
        first = c * kc
        start = pl.multiple_of(jnp.minimum(first, lp - kc), SEQ_BLOCK)
        kpos = start + koff
        mask = jnp.logical_and(kpos >= jnp.maximum(first, FRONT_PAD), kpos <= qpos)
        kb = k_ref[0, pl.ds(start, kc), :].astype(BF16)
        vb = v_ref[0, pl.ds(start, kc), :].astype(BF16)
        out = []
        for h in range(2):
            m, l, acc = carry[h]
            s = _dot_nt(qh[h], kb) + (cq[h] - crow_ref[0, 0, pl.ds(h, 1), pl.ds(start, kc)])
            s = jnp.where(mask, s, NEG_BIG)
            m_new = jnp.maximum(m, jnp.max(s, axis=-1, keepdims=True))
            p = jnp.exp(s - m_new)
            corr = jnp.exp(m - m_new)
            out.append((m_new, l * corr + jnp.sum(p, axis=-1, keepdims=True),
                        acc * corr + _dot(p.astype(BF16), vb)))
        return tuple(out)

    init = tuple((jnp.full((tq, 1), NEG_BIG, F32), jnp.zeros((tq, 1), F32), jnp.zeros((tq, LANES), F32))
                 for _ in range(2))
    carry = lax.fori_loop(0, ((qi + 2) * tq + kc - 1) // kc, chunk, init)
    o_ref[0] = jnp.where(lo, carry[0][2] / carry[0][1], carry[1][2] / carry[1][1])


def _attention(p3, ccol, crow, *, nq, kcol, vcol):
    b, lp, _ = p3.shape
    nhp = ccol.shape[1]
    kc = max(c for c in (4 * SEQ_BLOCK, 2 * SEQ_BLOCK, SEQ_BLOCK) if c <= lp)
    return pl.pallas_call(
        functools.partial(_attn_kernel, kc=kc),
        grid=(b, nhp, nq),
        in_specs=[
            pl.BlockSpec((1, SEQ_BLOCK, LANES), lambda i, h, q: (i, q + 1, h)),
            pl.BlockSpec((1, lp, LANES), lambda i, h, q: (i, 0, kcol + h)),
            pl.BlockSpec((1, lp, LANES), lambda i, h, q: (i, 0, vcol + h)),
            pl.BlockSpec((1, 1, SEQ_BLOCK, 2), lambda i, h, q: (i, h, q + 1, 0)),
            pl.BlockSpec((1, 1, 2, lp), lambda i, h, q: (i, h, 0, 0)),
        ],
        out_specs=pl.BlockSpec((1, SEQ_BLOCK, LANES), lambda i, h, q: (i, q, h)),
        out_shape=jax.ShapeDtypeStruct((b, nq * SEQ_BLOCK, nhp * LANES), F32),
        compiler_params=_cparams("arbitrary", "arbitrary", "arbitrary"),
        name="attn",
    )(p3, p3, p3, ccol, crow)


def _pagesuf_kernel(lf_ref, suf_ref, tot_ref):
    n = lf_ref.shape[2]
    r = lax.broadcasted_iota(jnp.int32, (n, n), 0)
    c = lax.broadcasted_iota(jnp.int32, (n, n), 1)
    lower = (r > c).astype(BF16)

    def body(i, carry):
        x = lf_ref[i]
        a, b_, c_ = _split3(x)
        suf = _dot(a, lower) + _dot(b_, lower) + _dot(c_, lower)
        suf_ref[i] = suf
        tot_ref[i] = jnp.broadcast_to(suf[:, 0:1] + x[:, 0:1], x.shape)
        return carry

    lax.fori_loop(0, lf_ref.shape[0], body, 0)


def _page_suffix(lf):
    n, nh, ps = lf.shape
    pb = max(c for c in (32, 16, 8, 4, 2, 1) if n % c == 0)
    spec = pl.BlockSpec((pb, nh, ps), lambda i: (i, 0, 0))
    return pl.pallas_call(
        _pagesuf_kernel,
        grid=(n // pb,),
        in_specs=[spec],
        out_specs=[spec, spec],
        out_shape=[jax.ShapeDtypeStruct(lf.shape, F32)] * 2,
        compiler_params=_cparams("arbitrary"),
        name="pagesuf",
    )(lf)


def _decattn_kernel(pt_ref, q_ref, knew_ref, vnew_ref, bnew_ref, tnew_ref, ck_ref, cv_ref, suf_ref, tot_ref,
                    o_ref, m_ref, l_ref, acc_ref, car_ref, *, n_new, n_heads):
    step = pl.program_id(1)
    tq = SEQ_BLOCK
    tpad = tq // n_heads
    ps = ck_ref.shape[2]

    def process(kt, vt, bias_h, valid, first):
        bias = jnp.broadcast_to(bias_h[:, None, :], (n_heads, tpad, ps)).reshape(tq, ps)
        s = _dot(q_ref[0], kt.astype(BF16)) + bias
        if valid is not None:
            s = jnp.where(valid, s, NEG_BIG)
        smax = jnp.max(s, axis=-1, keepdims=True)
        m_new = smax if first else jnp.maximum(m_ref[...], smax)
        p = jnp.exp(s - m_new)
        psum = jnp.sum(p, axis=-1, keepdims=True)
        pv = _dot_nt(p.astype(BF16), vt.astype(BF16))
        if first:
            acc_ref[...] = pv
            l_ref[...] = psum
        else:
            corr = jnp.exp(m_ref[...] - m_new)
            acc_ref[...] = acc_ref[...] * corr + pv
            l_ref[...] = l_ref[...] * corr + psum
        m_ref[...] = m_new

    @pl.when(step == 0)
    def _():
        t = lax.broadcasted_iota(jnp.int32, (tq, ps), 0) % tpad
        key = lax.broadcasted_iota(jnp.int32, (tq, ps), 1)
        process(knew_ref[0], vnew_ref[0], bnew_ref[0], jnp.logical_and(key <= t, key < n_new), True)
        car_ref[...] = tnew_ref[0]

    @pl.when(step > 0)
    def _():
        car = car_ref[...]
        process(ck_ref[0], cv_ref[0], suf_ref[0] + car, None, False)
        car_ref[...] = car + tot_ref[0]

    @pl.when(step == pl.num_programs(1) - 1)
    def _():
        inv = 1.0 / l_ref[...]
        lo = lax.broadcasted_iota(jnp.int32, (tpad, LANES), 1) < HEAD_DIM
        for j in range(n_heads // 2):
            cols = slice(j * LANES, (j + 1) * LANES)
            ra = slice(2 * j * tpad, (2 * j + 1) * tpad)
            rb = slice((2 * j + 1) * tpad, (2 * j + 2) * tpad)
            o_ref[0, :, cols] = jnp.where(lo, acc_ref[ra, cols] * inv[ra], acc_ref[rb, cols] * inv[rb])


def _decode_attention(page_table, q, knew, vnew, bnew, tnew, ck, cv, suf, tot, *, n_new, n_heads):
    bd, n_pages = page_table.shape
    width, ps = ck.shape[1], ck.shape[2]
    tpad = SEQ_BLOCK // n_heads

    def page(i, s, pt):
        return (pt[i, n_pages - jnp.maximum(s, 1)], 0, 0)

    mine = lambda i, s, pt: (i, 0, 0)
    grid_spec = pltpu.PrefetchScalarGridSpec(
        num_scalar_prefetch=1,
        grid=(bd, n_pages + 1),
        in_specs=[
            pl.BlockSpec((1, SEQ_BLOCK, width), mine),
            pl.BlockSpec((1, width, ps), mine),
            pl.BlockSpec((1, width, ps), mine),
            pl.BlockSpec((1, n_heads, ps), mine),
            pl.BlockSpec((1, n_heads, ps), mine),
            pl.BlockSpec((1, width, ps), page),
            pl.BlockSpec((1, width, ps), page),
            pl.BlockSpec((1, n_heads, ps), page),
            pl.BlockSpec((1, n_heads, ps), page),
        ],
        out_specs=pl.BlockSpec((1, tpad, width), mine),
        scratch_shapes=[pltpu.VMEM((SEQ_BLOCK, 1), F32), pltpu.VMEM((SEQ_BLOCK, 1), F32),
                        pltpu.VMEM((SEQ_BLOCK, width), F32), pltpu.VMEM((n_heads, ps), F32)],
    )
    return pl.pallas_call(
        functools.partial(_decattn_kernel, n_new=n_new, n_heads=n_heads),
        grid_spec=grid_spec,
        out_shape=jax.ShapeDtypeStruct((bd, tpad, width), F32),
        compiler_params=_cparams("arbitrary", "arbitrary"),
        name="decattn",
    )(page_table, q, knew, vnew, bnew, tnew, ck, cv, suf, tot)


def _rwprep_kernel(*refs, seq_len, has_start):
    n_in = 4 + (4 if has_start else 0)
    p_refs = refs[0:4]
    st_refs = refs[4:8] if has_start else None
    (mur, muk, muv, mul, w0, w2, a0, a2, g2, kkw, kaw, rkw, g_ref) = refs[n_in:n_in + 13]
    (r_o, w_o, k_o, v_o, nkk_o, kka_o, gate_o, bon_o) = refs[n_in + 13:n_in + 21]
    carry = refs[n_in + 21:n_in + 25]
    i = pl.program_id(0)
    tm = p_refs[0].shape[0]

    @pl.when(i == 0)
    def _():
        for cr in carry:
            cr[...] = jnp.zeros_like(cr)

    mixed = []
    for n, (pr, mu) in enumerate(zip(p_refs, (mur, muk, muv, mul))):
        p = pr[...]
        row = lax.broadcasted_iota(jnp.int32, p.shape, 0)
        prev = jnp.where(row == 0, carry[n][0:1, :], pltpu.roll(p, 1, axis=0))
        if has_start:
            prev = jnp.where(row % seq_len == 0, st_refs[n][...], prev)
        carry[n][0:1, :] = p[tm - 1:tm, :]
        mixed.append(p + mu[...] * (prev - p))
    r, k, v, lora = mixed
    gm = g_ref[...]
    wa = lora[:, 0:LANES]
    wlog = _log_sigmoid(w0[...] + _dot(jnp.tanh(wa).astype(BF16), w2[...])) - 0.5
    decay = jnp.exp(-jnp.exp(wlog))
    a = jax.nn.sigmoid(a0[...] + _dot(wa.astype(BF16), a2[...]))
    gate = _dot(jax.nn.sigmoid(lora[:, LANES:]).astype(BF16), g2[...])
    kk = k * kkw[...]
    kk = kk * lax.rsqrt(_segsum(kk * kk, gm) + 1e-12)
    k2 = k * (1.0 + (a - 1.0) * kaw[...])
    r_o[...] = r
    w_o[...] = decay
    k_o[...] = k2
    v_o[...] = v
    nkk_o[...] = -kk
    kka_o[...] = kk * a
    gate_o[...] = gate
    bon_o[...] = _segsum(r * k2 * rkw[...], gm) * v


def _rwprep(p2, starts, params, gm, *, tm, seq_len, rcol, lcol):
    rows = p2.shape[0]
    wid = gm.shape[0]
    lw = 2 * LANES
    has_start = starts is not None
    row_spec = lambda w, cb: pl.BlockSpec((tm, w), lambda i: (i, cb))
    in_specs = [row_spec(wid, rcol), row_spec(wid, rcol + 1), row_spec(wid, rcol + 2), row_spec(lw, lcol)]
    args = [p2, p2, p2, p2]
    if has_start:
        in_specs += [row_spec(wid, 0), row_spec(wid, 0), row_spec(wid, 0), row_spec(lw, 0)]
        args += list(starts)
    for prm in params:
        in_specs.append(pl.BlockSpec(prm.shape, lambda i: (0, 0)))
    in_specs.append(pl.BlockSpec(gm.shape, lambda i: (0, 0)))
    args += list(params) + [gm]
    return pl.pallas_call(
        functools.partial(_rwprep_kernel, seq_len=seq_len, has_start=has_start),
        grid=(rows // tm,),
        in_specs=in_specs,
        out_specs=[pl.BlockSpec((tm, wid), lambda i: (i, 0))] * 8,
        out_shape=[jax.ShapeDtypeStruct((rows, wid), F32)] * 8,
        scratch_shapes=[pltpu.VMEM((8, wid), F32)] * 3 + [pltpu.VMEM((8, lw), F32)],
        compiler_params=_cparams("arbitrary"),
        name="rwprep",
    )(*args)


def _scan_kernel(r_ref, w_ref, k_ref, v_ref, a_ref, b_ref, s0_ref, y_ref, sT_ref, s_ref, *, tb):
    tblk = pl.program_id(1)
    dk = s_ref.shape[0]

    @pl.when(tblk == 0)
    def _():
        s_ref[...] = s0_ref[...]

    def step(t, carry):
        sa = s_ref[0] * a_ref[t, pl.ds(0, 1), :]
        for kx in range(1, dk):
            sa = sa + s_ref[kx] * a_ref[t, pl.ds(kx, 1), :]
        vt = v_ref[t]
        y = None
        for kx in range(dk):
            sk = (s_ref[kx] * w_ref[t, pl.ds(kx, 1), :] + sa * b_ref[t, pl.ds(kx, 1), :]
                  + vt * k_ref[t, pl.ds(kx, 1), :])
            s_ref[kx] = sk
            yk = sk * r_ref[t, pl.ds(kx, 1), :]
            y = yk if y is None else y + yk
        y_ref[t] = y
        return carry

    lax.fori_loop(0, tb, step, 0)

    @pl.when(tblk == pl.num_programs(1) - 1)
    def _():
        sT_ref[...] = s_ref[...]


def _scan(seqs, s0, *, tb, t_off, n_steps):
    npairs = s0.shape[-1]
    dk, dv = s0.shape[0], s0.shape[1]
    seq_spec = pl.BlockSpec((tb, dk, LANES), lambda g, t: (t + t_off // tb, 0, g))
    st_spec = pl.BlockSpec((dk, dv, LANES), lambda g, t: (0, 0, g))
    return pl.pallas_call(
        functools.partial(_scan_kernel, tb=tb),
        grid=(npairs // LANES, n_steps // tb),
        in_specs=[seq_spec] * 6 + [st_spec],
        out_specs=[pl.BlockSpec((tb, dv, LANES), lambda g, t: (t, 0, g)), st_spec],
        out_shape=[jax.ShapeDtypeStruct((n_steps, dv, npairs), F32), jax.ShapeDtypeStruct(s0.shape, F32)],
        scratch_shapes=[pltpu.VMEM((dk, dv, LANES), F32)],
        compiler_params=_cparams("arbitrary", "arbitrary"),
        name="wkvscan",
    )(*seqs, s0)


def _mix_kernel(x_ref, o_ref, og_ref, y_ref, gate_ref, bon_ref, on_ref, lnw_ref, lnb_ref, n2_ref,
                wo_ref, g_ref, h_ref, hn_ref):
    gm = g_ref[...]
    half = o_ref.shape[1]
    o = o_ref[...]
    fox = (o * lax.rsqrt(_segsum(o * o, gm) * (1.0 / HEAD_DIM) + RMS_EPS) * on_ref[...]
           * jax.nn.sigmoid(og_ref[...]))
    y = y_ref[...]
    d = y - _segsum(y, gm) * (1.0 / HEAD_DIM)
    var = _segsum(d * d, gm) * (1.0 / HEAD_DIM)
    rw = (d * lax.rsqrt(var + GN_EPS) * lnw_ref[...] + lnb_ref[...] + bon_ref[...]) * gate_ref[...]
    h = (x_ref[...] + _dot(fox.astype(BF16), wo_ref[0:half, :]) + _dot(rw.astype(BF16), wo_ref[half:, :]))
    h_ref[...] = h
    ms = jnp.mean(h * h, axis=-1, keepdims=True)
    hn_ref[...] = h * lax.rsqrt(ms + RMS_EPS) * n2_ref[...]


def _mix(x, o, og_src, y, gate_src, bon_src, vecs, wo, gm, *, n_blocks, og_map, rw_map):
    tm = SEQ_BLOCK
    d = x.shape[1]
    half = d // 2
    blk = lambda w, f: pl.BlockSpec((tm, w), f)
    const = lambda a: pl.BlockSpec(a.shape, lambda i: (0, 0))
    return pl.pallas_call(
        _mix_kernel,
        grid=(n_blocks,),
        in_specs=[blk(d, lambda i: (i, 0)), blk(half, lambda i: (i, 0)), blk(half, og_map),
                  blk(half, lambda i: (i, 0)), blk(half, rw_map), blk(half, rw_map)]
        + [const(v) for v in vecs] + [const(wo), const(gm)],
        out_specs=[blk(d, lambda i: (i, 0))] * 2,
        out_shape=[jax.ShapeDtypeStruct((n_blocks * tm, d), F32)] * 2,
        compiler_params=_cparams("arbitrary"),
        name="mix",
    )(x, o, og_src, y, gate_src, bon_src, *vecs, wo, gm)


def _topk_rows(s_ref, n_rows, out_v, out_i, idx_ref=None):
    pos = lax.broadcasted_iota(jnp.int32, (n_rows, s_ref.shape[1]), 0)
    for t in range(PEER_TOPK):
        s = s_ref[...]
        m = jnp.max(s, axis=0, keepdims=True)
        am = jnp.min(jnp.where(s == m, pos, n_rows), axis=0, keepdims=True)
        hit = pos == am
        out_v[pl.ds(t, 1), :] = m
        if idx_ref is None:
            out_i[pl.ds(t, 1), :] = am
        else:
            out_i[pl.ds(t, 1), :] = jnp.max(jnp.where(hit, idx_ref[...], -1), axis=0, keepdims=True)
        s_ref[...] = jnp.where(hit, -jnp.inf, s)


def _peersel_kernel(hn_ref, wq_ref, keys_ref, eidx_ref, gate_ref, s_ref, sv_ref, si_ref, cand_ref, cidx_ref,
                    fv_ref, fi_ref):
    q = _dot(hn_ref[...].astype(BF16), wq_ref[...])
    for h in range(PEER_HEADS):
        qh = q[:, h * LANES:(h + 1) * LANES].astype(BF16)
        for half in range(2):
            s_ref[...] = _dot_nt(keys_ref[h, half], qh)
            _topk_rows(s_ref, PEER_N_KEYS, sv_ref.at[half], si_ref.at[half])
        sv1 = sv_ref[1]
        si1 = si_ref[1]
        for i in range(PEER_TOPK):
            rows = slice(i * PEER_TOPK, (i + 1) * PEER_TOPK)
            cand_ref[rows, :] = sv_ref[0, pl.ds(i, 1), :] + sv1
            cidx_ref[rows, :] = si_ref[0, pl.ds(i, 1), :] * PEER_N_KEYS + si1
        _topk_rows(cand_ref, PEER_TOPK * PEER_TOPK, fv_ref, fi_ref, idx_ref=cidx_ref)
        fv = fv_ref[...]
        e = jnp.exp(fv - fv[0:1, :])
        rows = slice(h * PEER_TOPK, (h + 1) * PEER_TOPK)
        gate_ref[rows, :] = e / jnp.sum(e, axis=0, keepdims=True)
        eidx_ref[rows, :] = fi_ref[...]


def _peer_select(hn, wq, keys):
    n, d = hn.shape
    tm = SEQ_BLOCK
    nsel = PEER_HEADS * PEER_TOPK
    ncand = PEER_TOPK * PEER_TOPK
    return pl.pallas_call(
        _peersel_kernel,
        grid=(n // tm,),
        in_specs=[pl.BlockSpec((tm, d), lambda i: (i, 0)),
                  pl.BlockSpec(wq.shape, lambda i: (0, 0)),
                  pl.BlockSpec(keys.shape, lambda i: (0, 0, 0, 0))],
        out_specs=[pl.BlockSpec((nsel, tm), lambda i: (0, i))] * 2,
        out_shape=[jax.ShapeDtypeStruct((nsel, n), jnp.int32), jax.ShapeDtypeStruct((nsel, n), F32)],
        scratch_shapes=[pltpu.VMEM((PEER_N_KEYS, tm), F32),
                        pltpu.VMEM((2, PEER_TOPK, tm), F32), pltpu.VMEM((2, PEER_TOPK, tm), jnp.int32),
                        pltpu.VMEM((ncand, tm), F32), pltpu.VMEM((ncand, tm), jnp.int32),
                        pltpu.VMEM((PEER_TOPK, tm), F32), pltpu.VMEM((PEER_TOPK, tm), jnp.int32)],
        compiler_params=_cparams("arbitrary"),
        name="peersel",
    )(hn, wq, keys)


def _peer_kernel(eidx_hbm, table_hbm, x_ref, gate_ref, hres_ref, out_ref, idx_smem, buf, part_ref, sem_idx, sems,
                 *, tb, nsel, rows_per_expert, nslot):
    i = pl.program_id(0)
    rpe = rows_per_expert
    nchunk = rpe // 2
    slot_rows = nsel * rpe
    cp = pltpu.make_async_copy(eidx_hbm.at[pl.ds(pl.multiple_of(i * (tb * nsel), tb * nsel), tb * nsel)],
                               idx_smem, sem_idx)
    cp.start()
    cp.wait()

    grp = 8
    ngrp = nsel // grp

    def issue(t, slot):
        def body(g, carry):
            for j in range(grp):
                kx = g * grp + j
                e = idx_smem[t * nsel + kx]
                pltpu.make_async_copy(table_hbm.at[pl.ds(pl.multiple_of(e * rpe, rpe), rpe), :],
                                      buf.at[slot, pl.ds(pl.multiple_of(kx * rpe, rpe), rpe), :],
                                      sems.at[slot]).start(priority=j % 2)
            return carry

        lax.fori_loop(0, ngrp, body, 0)

    def wait(slot):
        pltpu.make_async_copy(table_hbm.at[pl.ds(0, slot_rows), :], buf.at[slot], sems.at[slot]).wait()

    lane = lax.broadcasted_iota(jnp.int32, gate_ref.shape, 1)
    sub8 = lax.broadcasted_iota(jnp.int32, (8, LANES), 0)

    def fold(a, b, shift):
        low = (sub8 & shift) == 0
        return jnp.where(low, a + pltpu.roll(a, 8 - shift, axis=0), b + pltpu.roll(b, shift, axis=0))

    def sublane_sums(ts):
        ys = [fold(ts[j], ts[j + 4], 4) for j in range(4)]
        zs = [fold(ys[j], ys[j + 2], 2) for j in range(2)]
        return fold(zs[0], zs[1], 1)

    def compute(t, slot):
        xt = x_ref[t]

        def hidden(g, carry):
            base = pl.multiple_of(g * (grp * rpe), grp * rpe)
            ts = []
            for j in range(grp):
                p = buf[slot, pl.ds(base + j * rpe, nchunk), :] * xt
                acc = p[0:8]
                for s in range(8, nchunk, 8):
                    acc = acc + p[s:s + 8]
                ts.append(acc)
            part_ref[pl.ds(pl.multiple_of(g * grp, grp), grp), :] = sublane_sums(ts)
            return carry

        lax.fori_loop(0, ngrp, hidden, 0)
        hid = jnp.sum(part_ref[...], axis=-1, keepdims=True)
        gate = jnp.sum(jnp.where(lane == t, gate_ref[...], 0.0), axis=-1, keepdims=True)
        act = gate * (hid * (lax.erf(hid / np.sqrt(2.0).astype(np.float32)) + 1.0) / 2.0)
        part_ref[...] = jnp.broadcast_to(act, (nsel, LANES))

        def combine(g, accs):
            base = pl.multiple_of(g * (grp * rpe), grp * rpe)
            a8 = part_ref[pl.ds(pl.multiple_of(g * grp, grp), grp), :]
            accs = list(accs)
            for j in range(grp):
                accs[j % 4] = accs[j % 4] + buf[slot, pl.ds(base + j * rpe + nchunk, nchunk), :] * a8[j:j + 1, :]
            return tuple(accs)

        zero = jnp.zeros((nchunk, LANES), F32)
        accs = lax.fori_loop(0, ngrp, combine, (zero, zero, zero, zero))
        out_ref[t] = hres_ref[t] + ((accs[0] + accs[1]) + (accs[2] + accs[3]))

    for t in range(nslot - 1):
        issue(t, t)

    def body(t, carry):
        @pl.when(t + nslot - 1 < tb)
        def _():
            issue(t + nslot - 1, (t + nslot - 1) % nslot)

        slot = t % nslot
        wait(slot)
        compute(t, slot)
        return carry

    lax.fori_loop(0, tb, body, 0)


def _peer_combine(eidx_flat, table, hn, gate_t, hres, *, nsel, rows_per_expert):
    n, d = hn.shape
    tb = SEQ_BLOCK
    nslot = 4
    nchunk = d // LANES
    tok = pl.BlockSpec((tb, nchunk, LANES), lambda i: (i, 0, 0))
    out = pl.pallas_call(
        functools.partial(_peer_kernel, tb=tb, nsel=nsel, rows_per_expert=rows_per_expert, nslot=nslot),
        grid=(n // tb,),
        in_specs=[pl.BlockSpec(memory_space=pl.ANY),
                  pl.BlockSpec(memory_space=pl.ANY),
                  tok,
                  pl.BlockSpec((nsel, tb), lambda i: (0, i)),
                  tok],
        out_specs=tok,
        out_shape=jax.ShapeDtypeStruct((n, nchunk, LANES), F32),
        scratch_shapes=[pltpu.SMEM((tb * nsel,), jnp.int32),
                        pltpu.VMEM((nslot, nsel * rows_per_expert, LANES), F32),
                        pltpu.VMEM((nsel, LANES), F32),
                        pltpu.SemaphoreType.DMA(()),
                        pltpu.SemaphoreType.DMA((nslot,))],
        compiler_params=_cparams("arbitrary"),
        name="peer",
    )(eidx_flat, table, hn.reshape(n, nchunk, LANES), gate_t, hres.reshape(n, nchunk, LANES))
    return out.reshape(n, d)


def _to_pairs(a, b, l):
    nh = a.shape[1] // HEAD_DIM
    a = a.reshape(b, l, nh, HEAD_DIM).transpose(1, 3, 0, 2).reshape(l, HEAD_DIM, b * nh)
    pad = (-a.shape[-1]) % LANES
    return jnp.pad(a, ((0, 0), (0, 0), (0, pad))) if pad else a


def kernel(x_prompt, x_sample, cache_k, cache_v, cache_logf, state_wkv, state_shift, page_table, meta_tokens,
           norm1_w, w_in, fox_bf, fox_qn, fox_kn, fox_on, rw_mu, rw_w0, rw_w2, rw_a0, rw_a2, rw_g2, rw_kk,
           rw_ka, rw_rk, rw_ln_w, rw_ln_b, w_o, norm2_w, peer_wq, peer_keys, peer_u, peer_v):
    depth = w_in.shape[0]
    assert depth == 1, "single-layer step"
    bsz, seq, d = x_prompt.shape
    bd, tdec, _ = x_sample.shape
    half = d // 2
    nh = half // HEAD_DIM
    nhp = half // LANES
    lp = FRONT_PAD + N_META + seq
    nq = seq // SEQ_BLOCK
    lora_w = 2 * LANES
    assert seq % SEQ_BLOCK == 0 and rw_w2.shape[1] + rw_a2.shape[1] == LANES and rw_g2.shape[1] == LANES

    tn = lora_w
    w = w_in[0]
    fox_cols = 4 * half + nh
    wq_, wk_, wv_ = w[:, 0:half], w[:, half:2 * half], w[:, 2 * half:3 * half]
    wfl, wog = w[:, 3 * half:3 * half + nh], w[:, 3 * half + nh:fox_cols]
    wrw = w[:, fox_cols:]
    w_all = jnp.concatenate([wq_, wk_, wv_, wog, wrw, jnp.pad(wfl, ((0, 0), (0, tn - nh)))], axis=1).astype(BF16)
    ncol = w_all.shape[1]
    nblk = ncol // tn
    n_norm = 2 * half // tn
    j_fl = nblk - 1
    rep = tn // HEAD_DIM
    cvec = jnp.zeros((nblk, tn), F32)
    cvec = cvec.at[0:n_norm // 2].set(jnp.tile(fox_qn[0] * (HEAD_DIM ** -0.5), rep))
    cvec = cvec.at[n_norm // 2:n_norm].set(jnp.tile(fox_kn[0], rep))
    cvec = cvec.at[j_fl, 0:nh].set(fox_bf[0]).reshape(nblk, 1, tn)
    g_tn = _head_ones(tn)
    g_half = _head_ones(half)
    nw1 = norm1_w[0].reshape(1, d)
    col_k, col_v, col_og, col_rw = half, 2 * half, 3 * half, 4 * half
    col_lora = col_rw + 3 * half
    col_fl = col_lora + lora_w

    mu = rw_mu[0]
    row = lambda a: a.reshape(1, -1)
    zpad = jnp.zeros((LANES - rw_w2.shape[1], half), F32)
    rw_params = [row(mu[0:half]), row(mu[half:2 * half]), row(mu[2 * half:3 * half]), row(mu[3 * half:]),
                 row(rw_w0[0]), jnp.concatenate([rw_w2[0], zpad], 0).astype(BF16),
                 row(rw_a0[0]), jnp.concatenate([zpad, rw_a2[0]], 0).astype(BF16),
                 rw_g2[0].astype(BF16), row(rw_kk[0]), row(rw_ka[0]), row(rw_rk[0])]
    mix_vecs = [row(fox_on[0]), row(rw_ln_w[0]), row(rw_ln_b[0]), row(norm2_w[0])]
    wo = w_o[0].astype(BF16)
    wq_peer = peer_wq[0].astype(BF16)
    khalf = peer_keys.shape[-1]
    keys = peer_keys[0].astype(BF16)
    keys = jnp.stack([jnp.pad(keys[:, 0], ((0, 0), (0, 0), (0, LANES - khalf))),
                      jnp.pad(keys[:, 1], ((0, 0), (0, 0), (khalf, LANES - 2 * khalf))) ], axis=1)
    rpe = 2 * d // LANES
    table = jnp.concatenate([peer_u[0], peer_v[0]], axis=1).reshape(-1, LANES)
    nsel = PEER_HEADS * PEER_TOPK

    def peer(hn, hres):
        eidx_t, gate_t = _peer_select(hn, wq_peer, keys)
        return _peer_combine(eidx_t.T.reshape(-1), table, hn, gate_t, hres, nsel=nsel, rows_per_expert=rpe)

    hp = jnp.concatenate([jnp.zeros((bsz, FRONT_PAD, d), F32),
                          jnp.broadcast_to(meta_tokens.astype(F32)[None], (bsz, N_META, d)),
                          x_prompt.astype(F32)], axis=1).reshape(bsz * lp, d)
    tm_p = 1024 if (bsz * lp) % 1024 == 0 else SEQ_BLOCK
    pp = _inproj(hp, nw1, w_all, cvec, g_tn, tm=tm_p, tn=tn, n_norm=n_norm, j_fl=j_fl)
    pp3 = pp.reshape(bsz, lp, ncol)
    ccol, crow = _fcum(pp3, col_fl // LANES)
    ccol = ccol[:, :, 0:nh].reshape(bsz, lp, nhp, 2).transpose(0, 2, 1, 3)
    crow = crow[:, 0:nh, :].reshape(bsz, nhp, 2, lp)
    o_p = _attention(pp3, ccol, crow, nq=nq, kcol=col_k // LANES, vcol=col_v // LANES)
    o_p = o_p.reshape(bsz * seq, half)

    rw_p = _rwprep(pp, None, rw_params, g_half, tm=SEQ_BLOCK, seq_len=lp,
                   rcol=col_rw // half, lcol=col_lora // lora_w)
    seqs_p = [_to_pairs(a, bsz, lp) for a in rw_p[0:6]]
    npair_p = seqs_p[0].shape[-1]
    tb_p = N_META
    y_p, s_p = _scan(seqs_p, jnp.zeros((HEAD_DIM, HEAD_DIM, npair_p), F32),
                     tb=tb_p, t_off=FRONT_PAD, n_steps=N_META + seq)
    y_p = y_p[N_META:, :, 0:bsz * nh].reshape(seq, HEAD_DIM, bsz, nh).transpose(2, 0, 3, 1).reshape(bsz * seq, half)
    wkv_p = s_p[:, :, 0:bsz * nh].transpose(2, 1, 0).reshape(1, bsz, nh, HEAD_DIM, HEAD_DIM)

    blocks_per_seq = lp // SEQ_BLOCK
    in_rows = lambda i: (i // nq) * blocks_per_seq + 1 + i % nq
    h_p, hn_p = _mix(x_prompt.reshape(bsz * seq, d).astype(F32), o_p, pp, y_p, rw_p[6], rw_p[7], mix_vecs, wo,
                     g_half, n_blocks=bsz * nq,
                     og_map=lambda i: (in_rows(i), col_og // half), rw_map=lambda i: (in_rows(i), 0))
    y_prompt = peer(hn_p, h_p).reshape(bsz, seq, d)

    pr = pp3[:, FRONT_PAD:]
    k_prompt = pr[:, :, col_k:col_k + half].reshape(1, bsz, N_META + seq, nh, HEAD_DIM)
    v_prompt = pr[:, :, col_v:col_v + half].reshape(1, bsz, N_META + seq, nh, HEAD_DIM)
    logf_prompt = pr[:, :, col_fl:col_fl + nh].reshape(1, bsz, N_META + seq, nh)
    shift_prompt = pp3[:, -1, col_rw:col_lora + lora_w].reshape(1, bsz, -1)

    nrow_d = bd * tdec
    rows_d = -(-nrow_d // SEQ_BLOCK) * SEQ_BLOCK
    xs = x_sample.astype(F32).reshape(nrow_d, d)
    xs_pad = jnp.pad(xs, ((0, rows_d - nrow_d), (0, 0)))
    pd_ = _inproj(xs_pad, nw1, w_all, cvec, g_tn, tm=SEQ_BLOCK, tn=tn, n_norm=n_norm, j_fl=j_fl)
    pd3 = pd_[0:nrow_d].reshape(bd, tdec, ncol)
    tpad = SEQ_BLOCK // nh
    n_pool, page_size = cache_k.shape[1], cache_k.shape[2]
    pages = lambda c: c[0].astype(F32).transpose(0, 2, 3, 1).reshape(n_pool, half, page_size)
    q4 = pd3[:, :, 0:half].reshape(bd, tdec, nh, HEAD_DIM)
    q_d = jnp.eye(nh, dtype=F32)[None, :, None, :, None] * q4[:, None]
    q_d = jnp.pad(q_d, ((0, 0), (0, 0), (0, tpad - tdec), (0, 0), (0, 0))).reshape(bd, SEQ_BLOCK, half).astype(BF16)
    padk = ((0, 0), (0, 0), (0, page_size - tdec))
    knew = jnp.pad(pd3[:, :, col_k:col_k + half].transpose(0, 2, 1), padk)
    vnew = jnp.pad(pd3[:, :, col_v:col_v + half].transpose(0, 2, 1), padk)
    lnew = jnp.pad(pd3[:, :, col_fl:col_fl + nh].transpose(0, 2, 1), padk)
    bnew, tnew = _page_suffix(lnew)
    suf, tot = _page_suffix(cache_logf[0].astype(F32).transpose(0, 2, 1))
    o_d = _decode_attention(page_table, q_d, knew, vnew, bnew, tnew, pages(cache_k), pages(cache_v), suf, tot,
                            n_new=tdec, n_heads=nh)
    o_d = jnp.pad(o_d[:, 0:tdec].reshape(nrow_d, half), ((0, rows_d - nrow_d), (0, 0)))

    sh = state_shift[0].astype(F32)
    st = jnp.pad(jnp.repeat(sh, tdec, axis=0), ((0, rows_d - nrow_d), (0, 0)))
    starts = [st[:, 0:half], st[:, half:2 * half], st[:, 2 * half:3 * half], st[:, 3 * half:]]
    rw_d = _rwprep(pd_, starts, rw_params, g_half, tm=SEQ_BLOCK, seq_len=tdec,
                   rcol=col_rw // half, lcol=col_lora // lora_w)
    seqs_d = [_to_pairs(a[0:nrow_d], bd, tdec) for a in rw_d[0:6]]
    s0_d = state_wkv[0].astype(F32).reshape(bd * nh, HEAD_DIM, HEAD_DIM).transpose(2, 1, 0)
    pad_d = (-s0_d.shape[-1]) % LANES
    if pad_d:
        s0_d = jnp.pad(s0_d, ((0, 0), (0, 0), (0, pad_d)))
    y_d, s_d = _scan(seqs_d, s0_d, tb=tdec, t_off=0, n_steps=tdec)
    y_d = y_d[:, :, 0:bd * nh].reshape(tdec, HEAD_DIM, bd, nh).transpose(2, 0, 3, 1).reshape(nrow_d, half)
    y_d = jnp.pad(y_d, ((0, rows_d - nrow_d), (0, 0)))
    wkv_d = s_d[:, :, 0:bd * nh].transpose(2, 1, 0).reshape(1, bd, nh, HEAD_DIM, HEAD_DIM)
    h_d, hn_d = _mix(xs_pad, o_d, pd_, y_d, rw_d[6], rw_d[7], mix_vecs, wo, g_half, n_blocks=rows_d // SEQ_BLOCK,
                     og_map=lambda i: (i, col_og // half), rw_map=lambda i: (i, 0))
    y_sample = peer(hn_d, h_d)[0:nrow_d].reshape(bd, tdec, d)

    k_sample = pd3[:, :, col_k:col_k + half].reshape(1, bd, tdec, nh, HEAD_DIM)
    v_sample = pd3[:, :, col_v:col_v + half].reshape(1, bd, tdec, nh, HEAD_DIM)
    logf_sample = pd3[:, :, col_fl:col_fl + nh].reshape(1, bd, tdec, nh)
    shift_sample = pd3[:, -1, col_rw:col_lora + lora_w].reshape(1, bd, -1)

    return (y_prompt, y_sample, k_prompt, v_prompt, logf_prompt, wkv_p, shift_prompt,
            k_sample, v_sample, logf_sample, wkv_d, shift_sample)
```

```python
import functools

import numpy as np
import jax
import jax.numpy as jnp
from jax import lax
from jax.experimental import pallas as pl
from jax.experimental.pallas import tpu as pltpu

F32 = jnp.float32
BF16 = jnp.bfloat16

LANES = 128
HEAD_DIM = 64
N_META = 16
PEER_HEADS = 8
PEER_N_KEYS = 128
PEER_TOPK = 16
RMS_EPS = 1e-6
GN_EPS = HEAD_DIM * 1e-5
NEG_BIG = -1e30
SEQ_BLOCK = 128
FRONT_PAD = SEQ_BLOCK - N_META
VMEM_LIMIT = 48 * 1024 * 1024
PAGE_GROUP = 4


def _cparams(*sem):
    return pltpu.CompilerParams(dimension_semantics=sem, vmem_limit_bytes=VMEM_LIMIT)


def _dot(a, b):
    return jnp.dot(a, b, preferred_element_type=F32)


def _dot_nt(a, b):
    return lax.dot_general(a, b, (((1,), (1,)), ((), ())), preferred_element_type=F32)


def _split2(x):
    hi = x.astype(BF16)
    lo = (x - hi.astype(F32)).astype(BF16)
    return hi, lo


def _split3(x):
    hi = x.astype(BF16)
    r = x - hi.astype(F32)
    mid = r.astype(BF16)
    lo = (r - mid.astype(F32)).astype(BF16)
    return hi, mid, lo


def _segsum(x, g):
    hi, lo = _split2(x)
    return _dot(hi, g) + _dot(lo, g)


def _log_sigmoid(z):
    return jnp.minimum(z, 0.0) - jnp.log1p(jnp.exp(-jnp.abs(z)))


def _head_ones(n):
    i = np.arange(n) // HEAD_DIM
    return jnp.asarray(i[:, None] == i[None, :], dtype=BF16)


def _inproj_kernel(x_ref, nw_ref, w_ref, cvec_ref, g_ref, o_ref, xn_ref, *, n_norm, j_fl):
    j = pl.program_id(1)

    @pl.when(j == 0)
    def _():
        x = x_ref[...]
        ms = jnp.mean(x * x, axis=-1, keepdims=True)
        xn_ref[...] = (x * lax.rsqrt(ms + RMS_EPS) * nw_ref[...]).astype(BF16)

    acc = _dot(xn_ref[...], w_ref[...])
    vec = cvec_ref[0]

    @pl.when(j < n_norm)
    def _():
        ss = _segsum(acc * acc, g_ref[...])
        o_ref[...] = acc * lax.rsqrt(ss * (1.0 / HEAD_DIM) + RMS_EPS) * vec

    @pl.when(j == j_fl)
    def _():
        o_ref[...] = _log_sigmoid(acc + vec)

    @pl.when(jnp.logical_and(j >= n_norm, j != j_fl))
    def _():
        o_ref[...] = acc


def _inproj(x, nw, w, cvec, g, *, tm, tn, n_norm, j_fl):
    rows, d = x.shape
    cols = w.shape[1]
    return pl.pallas_call(
        functools.partial(_inproj_kernel, n_norm=n_norm, j_fl=j_fl),
        grid=(rows // tm, cols // tn),
        in_specs=[
            pl.BlockSpec((tm, d), lambda i, j: (i, 0)),
            pl.BlockSpec((1, d), lambda i, j: (0, 0)),
            pl.BlockSpec((d, tn), lambda i, j: (0, j)),
            pl.BlockSpec((1, 1, tn), lambda i, j: (j, 0, 0)),
            pl.BlockSpec((tn, tn), lambda i, j: (0, 0)),
        ],
        out_specs=pl.BlockSpec((tm, tn), lambda i, j: (i, j)),
        out_shape=jax.ShapeDtypeStruct((rows, cols), F32),
        scratch_shapes=[pltpu.VMEM((tm, d), BF16)],
        compiler_params=_cparams("arbitrary", "arbitrary"),
        name="inproj",
    )(x, nw, w, cvec, g)


def _fcum_kernel(lf_ref, ccol_ref, crow_ref, *, nblk):
    r = lax.broadcasted_iota(jnp.int32, (SEQ_BLOCK, SEQ_BLOCK), 0)
    c = lax.broadcasted_iota(jnp.int32, (SEQ_BLOCK, SEQ_BLOCK), 1)
    upper = (c > r).astype(BF16)
    lower = (r > c).astype(BF16)

    def body(n, carry):
        ccar, rcar = carry
        i = nblk - 1 - n
        off = pl.multiple_of(i * SEQ_BLOCK, SEQ_BLOCK)
        x = lf_ref[0, pl.ds(off, SEQ_BLOCK), :]
        h, m, l = _split3(x)
        suf = _dot(upper, h) + _dot(upper, m) + _dot(upper, l)
        ccol_ref[0, pl.ds(off, SEQ_BLOCK), :] = -(suf + ccar)
        xt = x.T
        ht, mt, lt = _split3(xt)
        suft = _dot(ht, lower) + _dot(mt, lower) + _dot(lt, lower)
        crow_ref[0, :, pl.ds(off, SEQ_BLOCK)] = -(suft + rcar)
        return (ccar + suf[0:1, :] + x[0:1, :], rcar + suft[:, 0:1] + xt[:, 0:1])

    lax.fori_loop(0, nblk, body, (jnp.zeros((1, LANES), F32), jnp.zeros((LANES, 1), F32)))


def _fcum(p3, col_block):
    b, lp, _ = p3.shape
    return pl.pallas_call(
        functools.partial(_fcum_kernel, nblk=lp // SEQ_BLOCK),
        grid=(b,),
        in_specs=[pl.BlockSpec((1, lp, LANES), lambda i: (i, 0, col_block))],
        out_specs=[pl.BlockSpec((1, lp, LANES), lambda i: (i, 0, 0)),
                   pl.BlockSpec((1, LANES, lp), lambda i: (i, 0, 0))],
        out_shape=[jax.ShapeDtypeStruct((b, lp, LANES), F32), jax.ShapeDtypeStruct((b, LANES, lp), F32)],
        compiler_params=_cparams("arbitrary"),
        name="fcum",
    )(p3)


def _attn_kernel(q_ref, k_ref, v_ref, ccol_ref, crow_ref, o_ref, *, kc):
    qi = pl.program_id(2)
    tq = SEQ_BLOCK
    lp = k_ref.shape[1]
    lo = lax.broadcasted_iota(jnp.int32, (tq, LANES), 1) < HEAD_DIM
    q = q_ref[0]
    qh = (jnp.where(lo, q, 0.0).astype(BF16), jnp.where(lo, 0.0, q).astype(BF16))
    cq = (ccol_ref[0, 0, :, 0:1], ccol_ref[0, 0, :, 1:2])
    qpos = (qi + 1) * tq + lax.broadcasted_iota(jnp.int32, (tq, kc), 0)
    koff = lax.broadcasted_iota(jnp.int32, (tq, kc), 1)

    def chunk(c, carry):
        first = c * kc
        start = pl.multiple_of(jnp.minimum(first, lp - kc), SEQ_BLOCK)
        kpos = start + koff
        mask = jnp.logical_and(kpos >= jnp.maximum(first, FRONT_PAD), kpos <= qpos)
        kb = k_ref[0, pl.ds(start, kc), :].astype(BF16)
        vb = v_ref[0, pl.ds(start, kc), :].astype(BF16)
        out = []
        for h in range(2):
            m, l, acc = carry[h]
            s = _dot_nt(qh[h], kb) + (cq[h] - crow_ref[0, 0, pl.ds(h, 1), pl.ds(start, kc)])
            s = jnp.where(mask, s, NEG_BIG)
            m_new = jnp.maximum(m, jnp.max(s, axis=-1, keepdims=True))
            p = jnp.exp(s - m_new)
            corr = jnp.exp(m - m_new)
            out.append((m_new, l * corr + jnp.sum(p, axis=-1, keepdims=True),
                        acc * corr + _dot(p.astype(BF16), vb)))
        return tuple(out)

    init = tuple((jnp.full((tq, 1), NEG_BIG, F32), jnp.zeros((tq, 1), F32), jnp.zeros((tq, LANES), F32))
                 for _ in range(2))
    carry = lax.fori_loop(0, ((qi + 2) * tq + kc - 1) // kc, chunk, init)
    o_ref[0] = jnp.where(lo, carry[0][2] / carry[0][1], carry[1][2] / carry[1][1])


def _attention(p3, ccol, crow, *, nq, kcol, vcol):
    b, lp, _ = p3.shape
    nhp = ccol.shape[1]
    kc = max(c for c in (4 * SEQ_BLOCK, 2 * SEQ_BLOCK, SEQ_BLOCK) if c <= lp)
    return pl.pallas_call(
        functools.partial(_attn_kernel, kc=kc),
        grid=(b, nhp, nq),
        in_specs=[
            pl.BlockSpec((1, SEQ_BLOCK, LANES), lambda i, h, q: (i, q + 1, h)),
            pl.BlockSpec((1, lp, LANES), lambda i, h, q: (i, 0, kcol + h)),
            pl.BlockSpec((1, lp, LANES), lambda i, h, q: (i, 0, vcol + h)),
            pl.BlockSpec((1, 1, SEQ_BLOCK, 2), lambda i, h, q: (i, h, q + 1, 0)),
            pl.BlockSpec((1, 1, 2, lp), lambda i, h, q: (i, h, 0, 0)),
        ],
        out_specs=pl.BlockSpec((1, SEQ_BLOCK, LANES), lambda i, h, q: (i, q, h)),
        out_shape=jax.ShapeDtypeStruct((b, nq * SEQ_BLOCK, nhp * LANES), F32),
        compiler_params=_cparams("arbitrary", "arbitrary", "arbitrary"),
        name="attn",
    )(p3, p3, p3, ccol, crow)


def _pagesuf_kernel(lf_ref, suf_ref, tot_ref):
    pb, nh, n = lf_ref.shape
    r = lax.broadcasted_iota(jnp.int32, (n, n), 0)
    c = lax.broadcasted_iota(jnp.int32, (n, n), 1)
    lower = (r > c).astype(BF16)
    x = lf_ref[...].reshape(pb * nh, n)
    a, b_, c_ = _split3(x)
    suf = _dot(a, lower) + _dot(b_, lower) + _dot(c_, lower)
    suf_ref[...] = suf.reshape(pb, nh, n)
    tot_ref[...] = jnp.broadcast_to(suf[:, 0:1] + x[:, 0:1], x.shape).reshape(pb, nh, n)


def _page_suffix(lf):
    n, nh, ps = lf.shape
    pb = max(c for c in (32, 16, 8, 4, 2, 1) if n % c == 0)
    spec = pl.BlockSpec((pb, nh, ps), lambda i: (i, 0, 0))
    return pl.pallas_call(
        _pagesuf_kernel,
        grid=(n // pb,),
        in_specs=[spec],
        out_specs=[spec, spec],
        out_shape=[jax.ShapeDtypeStruct(lf.shape, F32)] * 2,
        compiler_params=_cparams("arbitrary"),
        name="pagesuf",
    )(lf)


def _decattn_kernel(pt_ref, q_ref, knew_ref, vnew_ref, bnew_ref, tnew_ref, *refs, n_new, n_heads, group):
    page_refs = [refs[4 * g:4 * g + 4] for g in range(group)]
    o_ref, m_ref, l_ref, acc_ref, car_ref = refs[4 * group:]
    step = pl.program_id(1)
    tq = SEQ_BLOCK
    tpad = tq // n_heads
    ps = knew_ref.shape[2]

    def process(kts, vts, biases, valid, first):
        ss = []
        for kt, bias_h in zip(kts, biases):
            bias = jnp.broadcast_to(bias_h[:, None, :], (n_heads, tpad, ps)).reshape(tq, ps)
            s = _dot(q_ref[0], kt.astype(BF16)) + bias
            ss.append(s if valid is None else jnp.where(valid, s, NEG_BIG))
        smax = functools.reduce(jnp.maximum, [jnp.max(s, axis=-1, keepdims=True) for s in ss])
        m_new = smax if first else jnp.maximum(m_ref[...], smax)
        ps_ = [jnp.exp(s - m_new) for s in ss]
        psum = sum(jnp.sum(p, axis=-1, keepdims=True) for p in ps_)
        pv = sum(_dot_nt(p.astype(BF16), vt.astype(BF16)) for p, vt in zip(ps_, vts))
        if first:
            acc_ref[...] = pv
            l_ref[...] = psum
        else:
            corr = jnp.exp(m_ref[...] - m_new)
            acc_ref[...] = acc_ref[...] * corr + pv
            l_ref[...] = l_ref[...] * corr + psum
        m_ref[...] = m_new

    @pl.when(step == 0)
    def _():
        t = lax.broadcasted_iota(jnp.int32, (tq, ps), 0) % tpad
        key = lax.broadcasted_iota(jnp.int32, (tq, ps), 1)
        process([knew_ref[0]], [vnew_ref[0]], [bnew_ref[0]], jnp.logical_and(key <= t, key < n_new), True)
        car_ref[...] = tnew_ref[0]

    @pl.when(step > 0)
    def _():
        car = car_ref[...]
        biases = []
        for _, _, suf_ref, tot_ref in page_refs:
            biases.append(suf_ref[0] + car)
            car = car + tot_ref[0]
        process([r[0][0] for r in page_refs], [r[1][0] for r in page_refs], biases, None, False)
        car_ref[...] = car

    @pl.when(step == pl.num_programs(1) - 1)
    def _():
        inv = 1.0 / l_ref[...]
        lo = lax.broadcasted_iota(jnp.int32, (tpad, LANES), 1) < HEAD_DIM
        for j in range(n_heads // 2):
            cols = slice(j * LANES, (j + 1) * LANES)
            ra = slice(2 * j * tpad, (2 * j + 1) * tpad)
            rb = slice((2 * j + 1) * tpad, (2 * j + 2) * tpad)
            o_ref[0, :, cols] = jnp.where(lo, acc_ref[ra, cols] * inv[ra], acc_ref[rb, cols] * inv[rb])


def _decode_attention(page_table, q, knew, vnew, bnew, tnew, ck, cv, suf, tot, *, n_new, n_heads):
    bd, n_pages = page_table.shape
    width, ps = ck.shape[1], ck.shape[2]
    tpad = SEQ_BLOCK // n_heads
    group = max(g for g in (PAGE_GROUP, 2, 1) if n_pages % g == 0)

    def page(g):
        return lambda i, s, pt: (pt[i, n_pages - 1 - g - (jnp.maximum(s, 1) - 1) * group], 0, 0)

    mine = lambda i, s, pt: (i, 0, 0)
    in_specs = [pl.BlockSpec((1, SEQ_BLOCK, width), mine),
                pl.BlockSpec((1, width, ps), mine),
                pl.BlockSpec((1, width, ps), mine),
                pl.BlockSpec((1, n_heads, ps), mine),
                pl.BlockSpec((1, n_heads, ps), mine)]
    args = [q, knew, vnew, bnew, tnew]
    for g in range(group):
        in_specs += [pl.BlockSpec((1, width, ps), page(g)), pl.BlockSpec((1, width, ps), page(g)),
                     pl.BlockSpec((1, n_heads, ps), page(g)), pl.BlockSpec((1, n_heads, ps), page(g))]
        args += [ck, cv, suf, tot]
    grid_spec = pltpu.PrefetchScalarGridSpec(
        num_scalar_prefetch=1,
        grid=(bd, n_pages // group + 1),
        in_specs=in_specs,
        out_specs=pl.BlockSpec((1, tpad, width), mine),
        scratch_shapes=[pltpu.VMEM((SEQ_BLOCK, 1), F32), pltpu.VMEM((SEQ_BLOCK, 1), F32),
                        pltpu.VMEM((SEQ_BLOCK, width), F32), pltpu.VMEM((n_heads, ps), F32)],
    )
    return pl.pallas_call(
        functools.partial(_decattn_kernel, n_new=n_new, n_heads=n_heads, group=group),
        grid_spec=grid_spec,
        out_shape=jax.ShapeDtypeStruct((bd, tpad, width), F32),
        compiler_params=_cparams("arbitrary", "arbitrary"),
        name="decattn",
    )(page_table, *args)


def _rwprep_kernel(*refs, seq_len, has_start):
    n_in = 4 + (4 if has_start else 0)
    p_refs = refs[0:4]
    st_refs = refs[4:8] if has_start else None
    (mur, muk, muv, mul, w0, w2, a0, a2, g2, kkw, kaw, rkw, g_ref) = refs[n_in:n_in + 13]
    (r_o, w_o, k_o, v_o, nkk_o, kka_o, gate_o, bon_o) = refs[n_in + 13:n_in + 21]
    carry = refs[n_in + 21:n_in + 25]
    i = pl.program_id(0)
    tm = p_refs[0].shape[0]

    @pl.when(i == 0)
    def _():
        for cr in carry:
            cr[...] = jnp.zeros_like(cr)

    mixed = []
    for n, (pr, mu) in enumerate(zip(p_refs, (mur, muk, muv, mul))):
        p = pr[...]
        row = lax.broadcasted_iota(jnp.int32, p.shape, 0)
        prev = jnp.where(row == 0, carry[n][0:1, :], pltpu.roll(p, 1, axis=0))
        if has_start:
            prev = jnp.where(row % seq_len == 0, st_refs[n][...], prev)
        carry[n][0:1, :] = p[tm - 1:tm, :]
        mixed.append(p + mu[...] * (prev - p))
    r, k, v, lora = mixed
    gm = g_ref[...]
    wa = lora[:, 0:LANES]
    wlog = _log_sigmoid(w0[...] + _dot(jnp.tanh(wa).astype(BF16), w2[...])) - 0.5
    decay = jnp.exp(-jnp.exp(wlog))
    a = jax.nn.sigmoid(a0[...] + _dot(wa.astype(BF16), a2[...]))
    gate = _dot(jax.nn.sigmoid(lora[:, LANES:]).astype(BF16), g2[...])
    kk = k * kkw[...]
    kk = kk * lax.rsqrt(_segsum(kk * kk, gm) + 1e-12)
    k2 = k * (1.0 + (a - 1.0) * kaw[...])
    r_o[...] = r
    w_o[...] = decay
    k_o[...] = k2
    v_o[...] = v
    nkk_o[...] = -kk
    kka_o[...] = kk * a
    gate_o[...] = gate
    bon_o[...] = _segsum(r * k2 * rkw[...], gm) * v


def _rwprep(p2, starts, params, gm, *, tm, seq_len, rcol, lcol):
    rows = p2.shape[0]
    wid = gm.shape[0]
    lw = 2 * LANES
    has_start = starts is not None
    row_spec = lambda w, cb: pl.BlockSpec((tm, w), lambda i: (i, cb))
    in_specs = [row_spec(wid, rcol), row_spec(wid, rcol + 1), row_spec(wid, rcol + 2), row_spec(lw, lcol)]
    args = [p2, p2, p2, p2]
    if has_start:
        in_specs += [row_spec(wid, 0), row_spec(wid, 0), row_spec(wid, 0), row_spec(lw, 0)]
        args += list(starts)
    for prm in params:
        in_specs.append(pl.BlockSpec(prm.shape, lambda i: (0, 0)))
    in_specs.append(pl.BlockSpec(gm.shape, lambda i: (0, 0)))
    args += list(params) + [gm]
    return pl.pallas_call(
        functools.partial(_rwprep_kernel, seq_len=seq_len, has_start=has_start),
        grid=(rows // tm,),
        in_specs=in_specs,
        out_specs=[pl.BlockSpec((tm, wid), lambda i: (i, 0))] * 8,
        out_shape=[jax.ShapeDtypeStruct((rows, wid), F32)] * 8,
        scratch_shapes=[pltpu.VMEM((8, wid), F32)] * 3 + [pltpu.VMEM((8, lw), F32)],
        compiler_params=_cparams("arbitrary"),
        name="rwprep",
    )(*args)


def _scan_kernel(r_ref, w_ref, k_ref, v_ref, a_ref, b_ref, s0_ref, y_ref, sT_ref, s_ref, *, tb):
    tblk = pl.program_id(1)
    dk = s_ref.shape[0]

    @pl.when(tblk == 0)
    def _():
        s_ref[...] = s0_ref[...]

    def step(t, carry):
        sa = s_ref[0] * a_ref[t, pl.ds(0, 1), :]
        for kx in range(1, dk):
            sa = sa + s_ref[kx] * a_ref[t, pl.ds(kx, 1), :]
        vt = v_ref[t]
        y = None
        for kx in range(dk):
            sk = (s_ref[kx] * w_ref[t, pl.ds(kx, 1), :] + sa * b_ref[t, pl.ds(kx, 1), :]
                  + vt * k_ref[t, pl.ds(kx, 1), :])
            s_ref[kx] = sk
            yk = sk * r_ref[t, pl.ds(kx, 1), :]
            y = yk if y is None else y + yk
        y_ref[t] = y
        return carry

    lax.fori_loop(0, tb, step, 0)

    @pl.when(tblk == pl.num_programs(1) - 1)
    def _():
        sT_ref[...] = s_ref[...]


def _scan(seqs, s0, *, tb, t_off, n_steps):
    npairs = s0.shape[-1]
    dk, dv = s0.shape[0], s0.shape[1]
    seq_spec = pl.BlockSpec((tb, dk, LANES), lambda g, t: (t + t_off // tb, 0, g))
    st_spec = pl.BlockSpec((dk, dv, LANES), lambda g, t: (0, 0, g))
    return pl.pallas_call(
        functools.partial(_scan_kernel, tb=tb),
        grid=(npairs // LANES, n_steps // tb),
        in_specs=[seq_spec] * 6 + [st_spec],
        out_specs=[pl.BlockSpec((tb, dv, LANES), lambda g, t: (t, 0, g)), st_spec],
        out_shape=[jax.ShapeDtypeStruct((n_steps, dv, npairs), F32), jax.ShapeDtypeStruct(s0.shape, F32)],
        scratch_shapes=[pltpu.VMEM((dk, dv, LANES), F32)],
        compiler_params=_cparams("arbitrary", "arbitrary"),
        name="wkvscan",
    )(*seqs, s0)


def _mix_kernel(x_ref, o_ref, og_ref, y_ref, gate_ref, bon_ref, on_ref, lnw_ref, lnb_ref, n2_ref,
                wo_ref, g_ref, h_ref, hn_ref):
    gm = g_ref[...]
    half = o_ref.shape[1]
    o = o_ref[...]
    fox = (o * lax.rsqrt(_segsum(o * o, gm) * (1.0 / HEAD_DIM) + RMS_EPS) * on_ref[...]
           * jax.nn.sigmoid(og_ref[...]))
    y = y_ref[...]
    d = y - _segsum(y, gm) * (1.0 / HEAD_DIM)
    var = _segsum(d * d, gm) * (1.0 / HEAD_DIM)
    rw = (d * lax.rsqrt(var + GN_EPS) * lnw_ref[...] + lnb_ref[...] + bon_ref[...]) * gate_ref[...]
    h = (x_ref[...] + _dot(fox.astype(BF16), wo_ref[0:half, :]) + _dot(rw.astype(BF16), wo_ref[half:, :]))
    h_ref[...] = h
    ms = jnp.mean(h * h, axis=-1, keepdims=True)
    hn_ref[...] = h * lax.rsqrt(ms + RMS_EPS) * n2_ref[...]


def _mix(x, o, og_src, y, gate_src, bon_src, vecs, wo, gm, *, n_blocks, og_map, rw_map):
    tm = SEQ_BLOCK
    d = x.shape[1]
    half = d // 2
    blk = lambda w, f: pl.BlockSpec((tm, w), f)
    const = lambda a: pl.BlockSpec(a.shape, lambda i: (0, 0))
    return pl.pallas_call(
        _mix_kernel,
        grid=(n_blocks,),
        in_specs=[blk(d, lambda i: (i, 0)), blk(half, lambda i: (i, 0)), blk(half, og_map),
                  blk(half, lambda i: (i, 0)), blk(half, rw_map), blk(half, rw_map)]
        + [const(v) for v in vecs] + [const(wo), const(gm)],
        out_specs=[blk(d, lambda i: (i, 0))] * 2,
        out_shape=[jax.ShapeDtypeStruct((n_blocks * tm, d), F32)] * 2,
        compiler_params=_cparams("arbitrary"),
        name="mix",
    )(x, o, og_src, y, gate_src, bon_src, *vecs, wo, gm)


def _extract_max(s_ref, t, out_v, out_i, idx_ref=None):
    n_rows = s_ref.shape[0]
    pos = lax.broadcasted_iota(jnp.int32, s_ref.shape, 0)
    s = s_ref[...]
    m = jnp.max(s, axis=0, keepdims=True)
    am = jnp.min(jnp.where(s == m, pos, n_rows), axis=0, keepdims=True)
    hit = pos == am
    out_v[pl.ds(t, 1), :] = m
    if idx_ref is None:
        out_i[pl.ds(t, 1), :] = am
    else:
        out_i[pl.ds(t, 1), :] = jnp.max(jnp.where(hit, idx_ref[...], -1), axis=0, keepdims=True)
    s_ref[...] = jnp.where(hit, -jnp.inf, s)


_CAND_KEEP = [PEER_TOPK // (i + 1) for i in range(PEER_TOPK)]
_CAND_ROWS = [-(-n // 8) * 8 for n in _CAND_KEEP]
_CAND_OFF = [sum(_CAND_ROWS[:i]) for i in range(PEER_TOPK)]
_N_CAND = sum(_CAND_ROWS)


def _peersel_kernel(hn_ref, wq_ref, keys_ref, eidx_ref, gate_ref, s_ref, sv_ref, si_ref, cand_ref, cidx_ref,
                    fv_ref, fi_ref):
    q = _dot(hn_ref[...].astype(BF16), wq_ref[...])
    nprob = 2 * PEER_HEADS
    for h in range(PEER_HEADS):
        qh = q[:, h * LANES:(h + 1) * LANES].astype(BF16)
        for half in range(2):
            s_ref[2 * h + half] = _dot_nt(keys_ref[h, half], qh)
    for t in range(PEER_TOPK):
        for n in range(nprob):
            _extract_max(s_ref.at[n], t, sv_ref.at[n], si_ref.at[n])
    for h in range(PEER_HEADS):
        for i in range(PEER_TOPK):
            nr = _CAND_ROWS[i]
            rows = slice(_CAND_OFF[i], _CAND_OFF[i] + nr)
            keep = lax.broadcasted_iota(jnp.int32, (nr, sv_ref.shape[2]), 0) < _CAND_KEEP[i]
            cand = sv_ref[2 * h, pl.ds(i, 1), :] + sv_ref[2 * h + 1, 0:nr, :]
            cand_ref[h, rows, :] = jnp.where(keep, cand, -jnp.inf)
            cidx_ref[h, rows, :] = si_ref[2 * h, pl.ds(i, 1), :] * PEER_N_KEYS + si_ref[2 * h + 1, 0:nr, :]
    for t in range(PEER_TOPK):
        for h in range(PEER_HEADS):
            _extract_max(cand_ref.at[h], t, fv_ref.at[h], fi_ref.at[h], idx_ref=cidx_ref.at[h])
    for h in range(PEER_HEADS):
        fv = fv_ref[h]
        e = jnp.exp(fv - fv[0:1, :])
        rows = slice(h * PEER_TOPK, (h + 1) * PEER_TOPK)
        gate_ref[rows, :] = e / jnp.sum(e, axis=0, keepdims=True)
        eidx_ref[rows, :] = fi_ref[h]


def _peer_select(hn, wq, keys):
    n, d = hn.shape
    tm = SEQ_BLOCK
    nsel = PEER_HEADS * PEER_TOPK
    nprob = 2 * PEER_HEADS
    return pl.pallas_call(
        _peersel_kernel,
        grid=(n // tm,),
        in_specs=[pl.BlockSpec((tm, d), lambda i: (i, 0)),
                  pl.BlockSpec(wq.shape, lambda i: (0, 0)),
                  pl.BlockSpec(keys.shape, lambda i: (0, 0, 0, 0))],
        out_specs=[pl.BlockSpec((nsel, tm), lambda i: (0, i))] * 2,
        out_shape=[jax.ShapeDtypeStruct((nsel, n), jnp.int32), jax.ShapeDtypeStruct((nsel, n), F32)],
        scratch_shapes=[pltpu.VMEM((nprob, PEER_N_KEYS, tm), F32),
                        pltpu.VMEM((nprob, PEER_TOPK, tm), F32), pltpu.VMEM((nprob, PEER_TOPK, tm), jnp.int32),
                        pltpu.VMEM((PEER_HEADS, _N_CAND, tm), F32), pltpu.VMEM((PEER_HEADS, _N_CAND, tm), jnp.int32),
                        pltpu.VMEM((PEER_HEADS, PEER_TOPK, tm), F32),
                        pltpu.VMEM((PEER_HEADS, PEER_TOPK, tm), jnp.int32)],
        compiler_params=_cparams("arbitrary"),
        name="peersel",
    )(hn, wq, keys)


def _peer_kernel(eidx_hbm, table_hbm, x_ref, gate_ref, hres_ref, out_ref, idx_smem, buf, part_ref, sem_idx, sems,
                 *, tb, nsel, rows_per_expert, nslot):
    i = pl.program_id(0)
    rpe = rows_per_expert
    nchunk = rpe // 2
    slot_rows = nsel * rpe
    cp = pltpu.make_async_copy(eidx_hbm.at[pl.ds(pl.multiple_of(i * (tb * nsel), tb * nsel), tb * nsel)],
                               idx_smem, sem_idx)
    cp.start()
    cp.wait()

    grp = 8
    ngrp = nsel // grp

    def issue(t, slot):
        def body(g, carry):
            for j in range(grp):
                kx = g * grp + j
                e = idx_smem[t * nsel + kx]
                pltpu.make_async_copy(table_hbm.at[pl.ds(pl.multiple_of(e * rpe, rpe), rpe), :],
                                      buf.at[slot, pl.ds(pl.multiple_of(kx * rpe, rpe), rpe), :],
                                      sems.at[slot]).start(priority=j % 2)
            return carry

        lax.fori_loop(0, ngrp, body, 0)

    def wait(slot):
        pltpu.make_async_copy(table_hbm.at[pl.ds(0, slot_rows), :], buf.at[slot], sems.at[slot]).wait()

    lane = lax.broadcasted_iota(jnp.int32, gate_ref.shape, 1)
    sub8 = lax.broadcasted_iota(jnp.int32, (8, LANES), 0)

    def fold(a, b, shift):
        low = (sub8 & shift) == 0
        return jnp.where(low, a + pltpu.roll(a, 8 - shift, axis=0), b + pltpu.roll(b, shift, axis=0))

    def sublane_sums(ts):
        ys = [fold(ts[j], ts[j + 4], 4) for j in range(4)]
        zs = [fold(ys[j], ys[j + 2], 2) for j in range(2)]
        return fold(zs[0], zs[1], 1)

    def compute(t, slot):
        xt = x_ref[t]

        def hidden(g, carry):
            base = pl.multiple_of(g * (grp * rpe), grp * rpe)
            ts = []
            for j in range(grp):
                p = buf[slot, pl.ds(base + j * rpe, nchunk), :].astype(F32) * xt
                acc = p[0:8]
                for s in range(8, nchunk, 8):
                    acc = acc + p[s:s + 8]
                ts.append(acc)
            part_ref[pl.ds(pl.multiple_of(g * grp, grp), grp), :] = sublane_sums(ts)
            return carry

        lax.fori_loop(0, ngrp, hidden, 0)
        hid = jnp.sum(part_ref[...], axis=-1, keepdims=True)
        gate = jnp.sum(jnp.where(lane == t, gate_ref[...], 0.0), axis=-1, keepdims=True)
        act = gate * (hid * (lax.erf(hid / np.sqrt(2.0).astype(np.float32)) + 1.0) / 2.0)
        part_ref[...] = jnp.broadcast_to(act, (nsel, LANES))

        def combine(g, accs):
            base = pl.multiple_of(g * (grp * rpe), grp * rpe)
            a8 = part_ref[pl.ds(pl.multiple_of(g * grp, grp), grp), :]
            accs = list(accs)
            for j in range(grp):
                vrow = buf[slot, pl.ds(base + j * rpe + nchunk, nchunk), :].astype(F32)
                accs[j % 4] = accs[j % 4] + vrow * a8[j:j + 1, :]
            return tuple(accs)

        zero = jnp.zeros((nchunk, LANES), F32)
        accs = lax.fori_loop(0, ngrp, combine, (zero, zero, zero, zero))
        out_ref[t] = hres_ref[t] + ((accs[0] + accs[1]) + (accs[2] + accs[3]))

    for t in range(nslot - 1):
        issue(t, t)

    def body(t, carry):
        @pl.when(t + nslot - 1 < tb)
        def _():
            issue(t + nslot - 1, (t + nslot - 1) % nslot)

        slot = t % nslot
        wait(slot)
        compute(t, slot)
        return carry

    lax.fori_loop(0, tb, body, 0)


def _peer_combine(eidx_flat, table, hn, gate_t, hres, *, nsel, rows_per_expert):
    n, d = hn.shape
    tb = SEQ_BLOCK
    nslot = 4
    nchunk = d // LANES
    tok = pl.BlockSpec((tb, nchunk, LANES), lambda i: (i, 0, 0))
    out = pl.pallas_call(
        functools.partial(_peer_kernel, tb=tb, nsel=nsel, rows_per_expert=rows_per_expert, nslot=nslot),
        grid=(n // tb,),
        in_specs=[pl.BlockSpec(memory_space=pl.ANY),
                  pl.BlockSpec(memory_space=pl.ANY),
                  tok,
                  pl.BlockSpec((nsel, tb), lambda i: (0, i)),
                  tok],
        out_specs=tok,
        out_shape=jax.ShapeDtypeStruct((n, nchunk, LANES), F32),
        scratch_shapes=[pltpu.SMEM((tb * nsel,), jnp.int32),
                        pltpu.VMEM((nslot, nsel * rows_per_expert, LANES), table.dtype),
                        pltpu.VMEM((nsel, LANES), F32),
                        pltpu.SemaphoreType.DMA(()),
                        pltpu.SemaphoreType.DMA((nslot,))],
        compiler_params=_cparams("arbitrary"),
        name="peer",
    )(eidx_flat, table, hn.reshape(n, nchunk, LANES), gate_t, hres.reshape(n, nchunk, LANES))
    return out.reshape(n, d)


def _to_pairs(a, b, l):
    nh = a.shape[1] // HEAD_DIM
    a = a.reshape(b, l, nh, HEAD_DIM).transpose(1, 3, 0, 2).reshape(l, HEAD_DIM, b * nh)
    pad = (-a.shape[-1]) % LANES
    return jnp.pad(a, ((0, 0), (0, 0), (0, pad))) if pad else a


def kernel(x_prompt, x_sample, cache_k, cache_v, cache_logf, state_wkv, state_shift, page_table, meta_tokens,
           norm1_w, w_in, fox_bf, fox_qn, fox_kn, fox_on, rw_mu, rw_w0, rw_w2, rw_a0, rw_a2, rw_g2, rw_kk,
           rw_ka, rw_rk, rw_ln_w, rw_ln_b, w_o, norm2_w, peer_wq, peer_keys, peer_u, peer_v):
    depth = w_in.shape[0]
    assert depth == 1, "single-layer step"
    bsz, seq, d = x_prompt.shape
    bd, tdec, _ = x_sample.shape
    half = d // 2
    nh = half // HEAD_DIM
    nhp = half // LANES
    lp = FRONT_PAD + N_META + seq
    nq = seq // SEQ_BLOCK
    lora_w = 2 * LANES
    assert seq % SEQ_BLOCK == 0 and rw_w2.shape[1] + rw_a2.shape[1] == LANES and rw_g2.shape[1] == LANES

    tn = lora_w
    w = w_in[0]
    fox_cols = 4 * half + nh
    wq_, wk_, wv_ = w[:, 0:half], w[:, half:2 * half], w[:, 2 * half:3 * half]
    wfl, wog = w[:, 3 * half:3 * half + nh], w[:, 3 * half + nh:fox_cols]
    wrw = w[:, fox_cols:]
    w_all = jnp.concatenate([wq_, wk_, wv_, wog, wrw, jnp.pad(wfl, ((0, 0), (0, tn - nh)))], axis=1).astype(BF16)
    ncol = w_all.shape[1]
    nblk = ncol // tn
    n_norm = 2 * half // tn
    j_fl = nblk - 1
    rep = tn // HEAD_DIM
    cvec = jnp.zeros((nblk, tn), F32)
    cvec = cvec.at[0:n_norm // 2].set(jnp.tile(fox_qn[0] * (HEAD_DIM ** -0.5), rep))
    cvec = cvec.at[n_norm // 2:n_norm].set(jnp.tile(fox_kn[0], rep))
    cvec = cvec.at[j_fl, 0:nh].set(fox_bf[0]).reshape(nblk, 1, tn)
    g_tn = _head_ones(tn)
    g_half = _head_ones(half)
    nw1 = norm1_w[0].reshape(1, d)
    col_k, col_v, col_og, col_rw = half, 2 * half, 3 * half, 4 * half
    col_lora = col_rw + 3 * half
    col_fl = col_lora + lora_w

    mu = rw_mu[0]
    row = lambda a: a.reshape(1, -1)
    zpad = jnp.zeros((LANES - rw_w2.shape[1], half), F32)
    rw_params = [row(mu[0:half]), row(mu[half:2 * half]), row(mu[2 * half:3 * half]), row(mu[3 * half:]),
                 row(rw_w0[0]), jnp.concatenate([rw_w2[0], zpad], 0).astype(BF16),
                 row(rw_a0[0]), jnp.concatenate([zpad, rw_a2[0]], 0).astype(BF16),
                 rw_g2[0].astype(BF16), row(rw_kk[0]), row(rw_ka[0]), row(rw_rk[0])]
    mix_vecs = [row(fox_on[0]), row(rw_ln_w[0]), row(rw_ln_b[0]), row(norm2_w[0])]
    wo = w_o[0].astype(BF16)
    wq_peer = peer_wq[0].astype(BF16)
    khalf = peer_keys.shape[-1]
    keys = peer_keys[0].astype(BF16)
    keys = jnp.stack([jnp.pad(keys[:, 0], ((0, 0), (0, 0), (0, LANES - khalf))),
                      jnp.pad(keys[:, 1], ((0, 0), (0, 0), (khalf, LANES - 2 * khalf))) ], axis=1)
    rpe = 2 * d // LANES
    table = jnp.concatenate([peer_u[0], peer_v[0]], axis=1).astype(BF16).reshape(-1, LANES)
    nsel = PEER_HEADS * PEER_TOPK

    def peer(hn, hres):
        eidx_t, gate_t = _peer_select(hn, wq_peer, keys)
        return _peer_combine(eidx_t.T.reshape(-1), table, hn, gate_t, hres, nsel=nsel, rows_per_expert=rpe)

    hp = jnp.concatenate([jnp.zeros((bsz, FRONT_PAD, d), F32),
                          jnp.broadcast_to(meta_tokens.astype(F32)[None], (bsz, N_META, d)),
                          x_prompt.astype(F32)], axis=1).reshape(bsz * lp, d)
    tm_p = 1024 if (bsz * lp) % 1024 == 0 else SEQ_BLOCK
    pp = _inproj(hp, nw1, w_all, cvec, g_tn, tm=tm_p, tn=tn, n_norm=n_norm, j_fl=j_fl)
    pp3 = pp.reshape(bsz, lp, ncol)
    ccol, crow = _fcum(pp3, col_fl // LANES)
    ccol = ccol[:, :, 0:nh].reshape(bsz, lp, nhp, 2).transpose(0, 2, 1, 3)
    crow = crow[:, 0:nh, :].reshape(bsz, nhp, 2, lp)
    o_p = _attention(pp3, ccol, crow, nq=nq, kcol=col_k // LANES, vcol=col_v // LANES)
    o_p = o_p.reshape(bsz * seq, half)

    rw_p = _rwprep(pp, None, rw_params, g_half, tm=SEQ_BLOCK, seq_len=lp,
                   rcol=col_rw // half, lcol=col_lora // lora_w)
    seqs_p = [_to_pairs(a, bsz, lp) for a in rw_p[0:6]]
    npair_p = seqs_p[0].shape[-1]
    tb_p = N_META
    y_p, s_p = _scan(seqs_p, jnp.zeros((HEAD_DIM, HEAD_DIM, npair_p), F32),
                     tb=tb_p, t_off=FRONT_PAD, n_steps=N_META + seq)
    y_p = y_p[N_META:, :, 0:bsz * nh].reshape(seq, HEAD_DIM, bsz, nh).transpose(2, 0, 3, 1).reshape(bsz * seq, half)
    wkv_p = s_p[:, :, 0:bsz * nh].transpose(2, 1, 0).reshape(1, bsz, nh, HEAD_DIM, HEAD_DIM)

    blocks_per_seq = lp // SEQ_BLOCK
    in_rows = lambda i: (i // nq) * blocks_per_seq + 1 + i % nq
    h_p, hn_p = _mix(x_prompt.reshape(bsz * seq, d).astype(F32), o_p, pp, y_p, rw_p[6], rw_p[7], mix_vecs, wo,
                     g_half, n_blocks=bsz * nq,
                     og_map=lambda i: (in_rows(i), col_og // half), rw_map=lambda i: (in_rows(i), 0))
    y_prompt = peer(hn_p, h_p).reshape(bsz, seq, d)

    pr = pp3[:, FRONT_PAD:]
    k_prompt = pr[:, :, col_k:col_k + half].reshape(1, bsz, N_META + seq, nh, HEAD_DIM)
    v_prompt = pr[:, :, col_v:col_v + half].reshape(1, bsz, N_META + seq, nh, HEAD_DIM)
    logf_prompt = pr[:, :, col_fl:col_fl + nh].reshape(1, bsz, N_META + seq, nh)
    shift_prompt = pp3[:, -1, col_rw:col_lora + lora_w].reshape(1, bsz, -1)

    nrow_d = bd * tdec
    rows_d = -(-nrow_d // SEQ_BLOCK) * SEQ_BLOCK
    xs = x_sample.astype(F32).reshape(nrow_d, d)
    xs_pad = jnp.pad(xs, ((0, rows_d - nrow_d), (0, 0)))
    pd_ = _inproj(xs_pad, nw1, w_all, cvec, g_tn, tm=SEQ_BLOCK, tn=tn, n_norm=n_norm, j_fl=j_fl)
    pd3 = pd_[0:nrow_d].reshape(bd, tdec, ncol)
    tpad = SEQ_BLOCK // nh
    n_pool, page_size = cache_k.shape[1], cache_k.shape[2]
    pages = lambda c: c[0].astype(F32).transpose(0, 2, 3, 1).reshape(n_pool, half, page_size)
    q4 = pd3[:, :, 0:half].reshape(bd, tdec, nh, HEAD_DIM)
    q_d = jnp.eye(nh, dtype=F32)[None, :, None, :, None] * q4[:, None]
    q_d = jnp.pad(q_d, ((0, 0), (0, 0), (0, tpad - tdec), (0, 0), (0, 0))).reshape(bd, SEQ_BLOCK, half).astype(BF16)
    padk = ((0, 0), (0, 0), (0, page_size - tdec))
    knew = jnp.pad(pd3[:, :, col_k:col_k + half].transpose(0, 2, 1), padk)
    vnew = jnp.pad(pd3[:, :, col_v:col_v + half].transpose(0, 2, 1), padk)
    lnew = jnp.pad(pd3[:, :, col_fl:col_fl + nh].transpose(0, 2, 1), padk)
    bnew, tnew = _page_suffix(lnew)
    suf, tot = _page_suffix(cache_logf[0].astype(F32).transpose(0, 2, 1))
    o_d = _decode_attention(page_table, q_d, knew, vnew, bnew, tnew, pages(cache_k), pages(cache_v), suf, tot,
                            n_new=tdec, n_heads=nh)
    o_d = jnp.pad(o_d[:, 0:tdec].reshape(nrow_d, half), ((0, rows_d - nrow_d), (0, 0)))

    sh = state_shift[0].astype(F32)
    st = jnp.pad(jnp.repeat(sh, tdec, axis=0), ((0, rows_d - nrow_d), (0, 0)))
    starts = [st[:, 0:half], st[:, half:2 * half], st[:, 2 * half:3 * half], st[:, 3 * half:]]
    rw_d = _rwprep(pd_, starts, rw_params, g_half, tm=SEQ_BLOCK, seq_len=tdec,
                   rcol=col_rw // half, lcol=col_lora // lora_w)
    seqs_d = [_to_pairs(a[0:nrow_d], bd, tdec) for a in rw_d[0:6]]
    s0_d = state_wkv[0].astype(F32).reshape(bd * nh, HEAD_DIM, HEAD_DIM).transpose(2, 1, 0)
    pad_d = (-s0_d.shape[-1]) % LANES
    if pad_d:
        s0_d = jnp.pad(s0_d, ((0, 0), (0, 0), (0, pad_d)))
    y_d, s_d = _scan(seqs_d, s0_d, tb=tdec, t_off=0, n_steps=tdec)
    y_d = y_d[:, :, 0:bd * nh].reshape(tdec, HEAD_DIM, bd, nh).transpose(2, 0, 3, 1).reshape(nrow_d, half)
    y_d = jnp.pad(y_d, ((0, rows_d - nrow_d), (0, 0)))
    wkv_d = s_d[:, :, 0:bd * nh].transpose(2, 1, 0).reshape(1, bd, nh, HEAD_DIM, HEAD_DIM)
    h_d, hn_d = _mix(xs_pad, o_d, pd_, y_d, rw_d[6], rw_d[7], mix_vecs, wo, g_half, n_blocks=rows_d // SEQ_BLOCK,
                     og_map=lambda i: (i, col_og // half), rw_map=lambda i: (i, 0))
    y_sample = peer(hn_d, h_d)[0:nrow_d].reshape(bd, tdec, d)

    k_sample = pd3[:, :, col_k:col_k + half].reshape(1, bd, tdec, nh, HEAD_DIM)
    v_sample = pd3[:, :, col_v:col_v + half].reshape(1, bd, tdec, nh, HEAD_DIM)
    logf_sample = pd3[:, :, col_fl:col_fl + nh].reshape(1, bd, tdec, nh)
    shift_sample = pd3[:, -1, col_rw:col_lora + lora_w].reshape(1, bd, -1)

    return (y_prompt, y_sample, k_prompt, v_prompt, logf_prompt, wkv_p, shift_prompt,
            k_sample, v_sample, logf_sample, wkv_d, shift_sample)
```

```python
import functools

import numpy as np
import jax
import jax.numpy as jnp
from jax import lax
from jax.experimental import pallas as pl
from jax.experimental.pallas import tpu as pltpu

F32 = jnp.float32
BF16 = jnp.bfloat16

LANES = 128
HEAD_DIM = 64
N_META = 16
PEER_HEADS = 8
PEER_N_KEYS = 128
PEER_TOPK = 16
RMS_EPS = 1e-6
GN_EPS = HEAD_DIM * 1e-5
NEG_BIG = -1e30
SEQ_BLOCK = 128
FRONT_PAD = SEQ_BLOCK - N_META
VMEM_LIMIT = 48 * 1024 * 1024
PAGE_GROUP = 4


def _cparams(*sem):
    return pltpu.CompilerParams(dimension_semantics=sem, vmem_limit_bytes=VMEM_LIMIT)


def _dot(a, b):
    return jnp.dot(a, b, preferred_element_type=F32)


def _dot_nt(a, b):
    return lax.dot_general(a, b, (((1,), (1,)), ((), ())), preferred_element_type=F32)


def _split2(x):
    hi = x.astype(BF16)
    lo = (x - hi.astype(F32)).astype(BF16)
    return hi, lo


def _split3(x):
    hi = x.astype(BF16)
    r = x - hi.astype(F32)
    mid = r.astype(BF16)
    lo = (r - mid.astype(F32)).astype(BF16)
    return hi, mid, lo


def _segsum(x, g):
    hi, lo = _split2(x)
    return _dot(hi, g) + _dot(lo, g)


def _log_sigmoid(z):
    return jnp.minimum(z, 0.0) - jnp.log1p(jnp.exp(-jnp.abs(z)))


def _head_ones(n):
    i = np.arange(n) // HEAD_DIM
    return jnp.asarray(i[:, None] == i[None, :], dtype=BF16)


def _inproj_kernel(x_ref, nw_ref, w_ref, cvec_ref, g_ref, o_ref, xn_ref, *, n_norm, j_fl):
    j = pl.program_id(1)

    @pl.when(j == 0)
    def _():
        x = x_ref[...]
        ms = jnp.mean(x * x, axis=-1, keepdims=True)
        xn_ref[...] = (x * lax.rsqrt(ms + RMS_EPS) * nw_ref[...]).astype(BF16)

    acc = _dot(xn_ref[...], w_ref[...])
    vec = cvec_ref[0]

    @pl.when(j < n_norm)
    def _():
        ss = _segsum(acc * acc, g_ref[...])
        o_ref[...] = acc * lax.rsqrt(ss * (1.0 / HEAD_DIM) + RMS_EPS) * vec

    @pl.when(j == j_fl)
    def _():
        o_ref[...] = _log_sigmoid(acc + vec)

    @pl.when(jnp.logical_and(j >= n_norm, j != j_fl))
    def _():
        o_ref[...] = acc


def _inproj(x, nw, w, cvec, g, *, tm, tn, n_norm, j_fl):
    rows, d = x.shape
    cols = w.shape[1]
    return pl.pallas_call(
        functools.partial(_inproj_kernel, n_norm=n_norm, j_fl=j_fl),
        grid=(rows // tm, cols // tn),
        in_specs=[
            pl.BlockSpec((tm, d), lambda i, j: (i, 0)),
            pl.BlockSpec((1, d), lambda i, j: (0, 0)),
            pl.BlockSpec((d, tn), lambda i, j: (0, j)),
            pl.BlockSpec((1, 1, tn), lambda i, j: (j, 0, 0)),
            pl.BlockSpec((tn, tn), lambda i, j: (0, 0)),
        ],
        out_specs=pl.BlockSpec((tm, tn), lambda i, j: (i, j)),
        out_shape=jax.ShapeDtypeStruct((rows, cols), F32),
        scratch_shapes=[pltpu.VMEM((tm, d), BF16)],
        compiler_params=_cparams("arbitrary", "arbitrary"),
        name="inproj",
    )(x, nw, w, cvec, g)


def _fcum_kernel(lf_ref, ccol_ref, crow_ref, *, nblk):
    r = lax.broadcasted_iota(jnp.int32, (SEQ_BLOCK, SEQ_BLOCK), 0)
    c = lax.broadcasted_iota(jnp.int32, (SEQ_BLOCK, SEQ_BLOCK), 1)
    upper = (c > r).astype(BF16)
    lower = (r > c).astype(BF16)

    def body(n, carry):
        ccar, rcar = carry
        i = nblk - 1 - n
        off = pl.multiple_of(i * SEQ_BLOCK, SEQ_BLOCK)
        x = lf_ref[0, pl.ds(off, SEQ_BLOCK), :]
        h, m, l = _split3(x)
        suf = _dot(upper, h) + _dot(upper, m) + _dot(upper, l)
        ccol_ref[0, pl.ds(off, SEQ_BLOCK), :] = -(suf + ccar)
        xt = x.T
        ht, mt, lt = _split3(xt)
        suft = _dot(ht, lower) + _dot(mt, lower) + _dot(lt, lower)
        crow_ref[0, :, pl.ds(off, SEQ_BLOCK)] = -(suft + rcar)
        return (ccar + suf[0:1, :] + x[0:1, :], rcar + suft[:, 0:1] + xt[:, 0:1])

    lax.fori_loop(0, nblk, body, (jnp.zeros((1, LANES), F32), jnp.zeros((LANES, 1), F32)))


def _fcum(p3, col_block):
    b, lp, _ = p3.shape
    return pl.pallas_call(
        functools.partial(_fcum_kernel, nblk=lp // SEQ_BLOCK),
        grid=(b,),
        in_specs=[pl.BlockSpec((1, lp, LANES), lambda i: (i, 0, col_block))],
        out_specs=[pl.BlockSpec((1, lp, LANES), lambda i: (i, 0, 0)),
                   pl.BlockSpec((1, LANES, lp), lambda i: (i, 0, 0))],
        out_shape=[jax.ShapeDtypeStruct((b, lp, LANES), F32), jax.ShapeDtypeStruct((b, LANES, lp), F32)],
        compiler_params=_cparams("arbitrary"),
        name="fcum",
    )(p3)


def _attn_kernel(q_ref, k_ref, v_ref, ccol_ref, crow_ref, o_ref, *, kc):
    qi = pl.program_id(2)
    tq = SEQ_BLOCK
    lp = k_ref.shape[1]
    lo = lax.broadcasted_iota(jnp.int32, (tq, LANES), 1) < HEAD_DIM
    q = q_ref[0]
    qh = (jnp.where(lo, q, 0.0).astype(BF16), jnp.where(lo, 0.0, q).astype(BF16))
    cq = (ccol_ref[0, 0, :, 0:1], ccol_ref[0, 0, :, 1:2])
    qpos = (qi + 1) * tq + lax.broadcasted_iota(jnp.int32, (tq, kc), 0)
    koff = lax.broadcasted_iota(jnp.int32, (tq, kc), 1)

    def chunk(c, carry):
        first = c * kc
        start = pl.multiple_of(jnp.minimum(first, lp - kc), SEQ_BLOCK)
        kpos = start + koff
        mask = jnp.logical_and(kpos >= jnp.maximum(first, FRONT_PAD), kpos <= qpos)
        kb = k_ref[0, pl.ds(start, kc), :].astype(BF16)
        vb = v_ref[0, pl.ds(start, kc), :].astype(BF16)
        out = []
        for h in range(2):
            m, l, acc = carry[h]
            s = _dot_nt(qh[h], kb) + (cq[h] - crow_ref[0, 0, pl.ds(h, 1), pl.ds(start, kc)])
            s = jnp.where(mask, s, NEG_BIG)
            m_new = jnp.maximum(m, jnp.max(s, axis=-1, keepdims=True))
            p = jnp.exp(s - m_new)
            corr = jnp.exp(m - m_new)
            out.append((m_new, l * corr + jnp.sum(p, axis=-1, keepdims=True),
                        acc * corr + _dot(p.astype(BF16), vb)))
        return tuple(out)

    init = tuple((jnp.full((tq, 1), NEG_BIG, F32), jnp.zeros((tq, 1), F32), jnp.zeros((tq, LANES), F32))
                 for _ in range(2))
    carry = lax.fori_loop(0, ((qi + 2) * tq + kc - 1) // kc, chunk, init)
    o_ref[0] = jnp.where(lo, carry[0][2] / carry[0][1], carry[1][2] / carry[1][1])


def _attention(p3, ccol, crow, *, nq, kcol, vcol):
    b, lp, _ = p3.shape
    nhp = ccol.shape[1]
    kc = max(c for c in (4 * SEQ_BLOCK, 2 * SEQ_BLOCK, SEQ_BLOCK) if c <= lp)
    return pl.pallas_call(
        functools.partial(_attn_kernel, kc=kc),
        grid=(b, nhp, nq),
        in_specs=[
            pl.BlockSpec((1, SEQ_BLOCK, LANES), lambda i, h, q: (i, q + 1, h)),
            pl.BlockSpec((1, lp, LANES), lambda i, h, q: (i, 0, kcol + h)),
            pl.BlockSpec((1, lp, LANES), lambda i, h, q: (i, 0, vcol + h)),
            pl.BlockSpec((1, 1, SEQ_BLOCK, 2), lambda i, h, q: (i, h, q + 1, 0)),
            pl.BlockSpec((1, 1, 2, lp), lambda i, h, q: (i, h, 0, 0)),
        ],
        out_specs=pl.BlockSpec((1, SEQ_BLOCK, LANES), lambda i, h, q: (i, q, h)),
        out_shape=jax.ShapeDtypeStruct((b, nq * SEQ_BLOCK, nhp * LANES), F32),
        compiler_params=_cparams("arbitrary", "arbitrary", "arbitrary"),
        name="attn",
    )(p3, p3, p3, ccol, crow)


def _pagesuf_kernel(lf_ref, suf_ref, tot_ref):
    pb, nh, n = lf_ref.shape
    r = lax.broadcasted_iota(jnp.int32, (n, n), 0)
    c = lax.broadcasted_iota(jnp.int32, (n, n), 1)
    lower = (r > c).astype(BF16)
    x = lf_ref[...].reshape(pb * nh, n)
    a, b_, c_ = _split3(x)
    suf = _dot(a, lower) + _dot(b_, lower) + _dot(c_, lower)
    suf_ref[...] = suf.reshape(pb, nh, n)
    tot_ref[...] = jnp.broadcast_to(suf[:, 0:1] + x[:, 0:1], x.shape).reshape(pb, nh, n)


def _page_suffix(lf):
    n, nh, ps = lf.shape
    pb = max(c for c in (32, 16, 8, 4, 2, 1) if n % c == 0)
    spec = pl.BlockSpec((pb, nh, ps), lambda i: (i, 0, 0))
    return pl.pallas_call(
        _pagesuf_kernel,
        grid=(n // pb,),
        in_specs=[spec],
        out_specs=[spec, spec],
        out_shape=[jax.ShapeDtypeStruct(lf.shape, F32)] * 2,
        compiler_params=_cparams("arbitrary"),
        name="pagesuf",
    )(lf)


def _decattn_kernel(pt_ref, q_ref, knew_ref, vnew_ref, bnew_ref, tnew_ref, *refs, n_new, n_heads, group):
    page_refs = [refs[4 * g:4 * g + 4] for g in range(group)]
    o_ref, m_ref, l_ref, acc_ref, car_ref = refs[4 * group:]
    step = pl.program_id(1)
    tq = SEQ_BLOCK
    tpad = tq // n_heads
    ps = knew_ref.shape[2]

    def process(kts, vts, biases, valid, first):
        ss = []
        for kt, bias_h in zip(kts, biases):
            bias = jnp.broadcast_to(bias_h[:, None, :], (n_heads, tpad, ps)).reshape(tq, ps)
            s = _dot(q_ref[0], kt.astype(BF16)) + bias
            ss.append(s if valid is None else jnp.where(valid, s, NEG_BIG))
        smax = functools.reduce(jnp.maximum, [jnp.max(s, axis=-1, keepdims=True) for s in ss])
        m_new = smax if first else jnp.maximum(m_ref[...], smax)
        ps_ = [jnp.exp(s - m_new) for s in ss]
        psum = sum(jnp.sum(p, axis=-1, keepdims=True) for p in ps_)
        pv = sum(_dot_nt(p.astype(BF16), vt.astype(BF16)) for p, vt in zip(ps_, vts))
        if first:
            acc_ref[...] = pv
            l_ref[...] = psum
        else:
            corr = jnp.exp(m_ref[...] - m_new)
            acc_ref[...] = acc_ref[...] * corr + pv
            l_ref[...] = l_ref[...] * corr + psum
        m_ref[...] = m_new

    @pl.when(step == 0)
    def _():
        t = lax.broadcasted_iota(jnp.int32, (tq, ps), 0) % tpad
        key = lax.broadcasted_iota(jnp.int32, (tq, ps), 1)
        process([knew_ref[0]], [vnew_ref[0]], [bnew_ref[0]], jnp.logical_and(key <= t, key < n_new), True)
        car_ref[...] = tnew_ref[0]

    @pl.when(step > 0)
    def _():
        car = car_ref[...]
        biases = []
        for _, _, suf_ref, tot_ref in page_refs:
            biases.append(suf_ref[0] + car)
            car = car + tot_ref[0]
        process([r[0][0] for r in page_refs], [r[1][0] for r in page_refs], biases, None, False)
        car_ref[...] = car

    @pl.when(step == pl.num_programs(1) - 1)
    def _():
        inv = 1.0 / l_ref[...]
        lo = lax.broadcasted_iota(jnp.int32, (tpad, LANES), 1) < HEAD_DIM
        for j in range(n_heads // 2):
            cols = slice(j * LANES, (j + 1) * LANES)
            ra = slice(2 * j * tpad, (2 * j + 1) * tpad)
            rb = slice((2 * j + 1) * tpad, (2 * j + 2) * tpad)
            o_ref[0, :, cols] = jnp.where(lo, acc_ref[ra, cols] * inv[ra], acc_ref[rb, cols] * inv[rb])


def _decode_attention(page_table, q, knew, vnew, bnew, tnew, ck, cv, suf, tot, *, n_new, n_heads):
    bd, n_pages = page_table.shape
    width, ps = ck.shape[1], ck.shape[2]
    tpad = SEQ_BLOCK // n_heads
    group = max(g for g in (PAGE_GROUP, 2, 1) if n_pages % g == 0)

    def page(g):
        return lambda i, s, pt: (pt[i, n_pages - 1 - g - (jnp.maximum(s, 1) - 1) * group], 0, 0)

    mine = lambda i, s, pt: (i, 0, 0)
    in_specs = [pl.BlockSpec((1, SEQ_BLOCK, width), mine),
                pl.BlockSpec((1, width, ps), mine),
                pl.BlockSpec((1, width, ps), mine),
                pl.BlockSpec((1, n_heads, ps), mine),
                pl.BlockSpec((1, n_heads, ps), mine)]
    args = [q, knew, vnew, bnew, tnew]
    for g in range(group):
        in_specs += [pl.BlockSpec((1, width, ps), page(g)), pl.BlockSpec((1, width, ps), page(g)),
                     pl.BlockSpec((1, n_heads, ps), page(g)), pl.BlockSpec((1, n_heads, ps), page(g))]
        args += [ck, cv, suf, tot]
    grid_spec = pltpu.PrefetchScalarGridSpec(
        num_scalar_prefetch=1,
        grid=(bd, n_pages // group + 1),
        in_specs=in_specs,
        out_specs=pl.BlockSpec((1, tpad, width), mine),
        scratch_shapes=[pltpu.VMEM((SEQ_BLOCK, 1), F32), pltpu.VMEM((SEQ_BLOCK, 1), F32),
                        pltpu.VMEM((SEQ_BLOCK, width), F32), pltpu.VMEM((n_heads, ps), F32)],
    )
    return pl.pallas_call(
        functools.partial(_decattn_kernel, n_new=n_new, n_heads=n_heads, group=group),
        grid_spec=grid_spec,
        out_shape=jax.ShapeDtypeStruct((bd, tpad, width), F32),
        compiler_params=_cparams("arbitrary", "arbitrary"),
        name="decattn",
    )(page_table, *args)


def _rwprep_kernel(*refs, seq_len, has_start):
    n_in = 4 + (4 if has_start else 0)
    p_refs = refs[0:4]
    st_refs = refs[4:8] if has_start else None
    (mur, muk, muv, mul, w0, w2, a0, a2, g2, kkw, kaw, rkw, g_ref) = refs[n_in:n_in + 13]
    (r_o, w_o, k_o, v_o, nkk_o, kka_o, gate_o, bon_o) = refs[n_in + 13:n_in + 21]
    carry = refs[n_in + 21:n_in + 25]
    i = pl.program_id(0)
    tm = p_refs[0].shape[0]

    @pl.when(i == 0)
    def _():
        for cr in carry:
            cr[...] = jnp.zeros_like(cr)

    mixed = []
    for n, (pr, mu) in enumerate(zip(p_refs, (mur, muk, muv, mul))):
        p = pr[...]
        row = lax.broadcasted_iota(jnp.int32, p.shape, 0)
        prev = jnp.where(row == 0, carry[n][0:1, :], pltpu.roll(p, 1, axis=0))
        if has_start:
            prev = jnp.where(row % seq_len == 0, st_refs[n][...], prev)
        carry[n][0:1, :] = p[tm - 1:tm, :]
        mixed.append(p + mu[...] * (prev - p))
    r, k, v, lora = mixed
    gm = g_ref[...]
    wa = lora[:, 0:LANES]
    wlog = _log_sigmoid(w0[...] + _dot(jnp.tanh(wa).astype(BF16), w2[...])) - 0.5
    decay = jnp.exp(-jnp.exp(wlog))
    a = jax.nn.sigmoid(a0[...] + _dot(wa.astype(BF16), a2[...]))
    gate = _dot(jax.nn.sigmoid(lora[:, LANES:]).astype(BF16), g2[...])
    kk = k * kkw[...]
    kk = kk * lax.rsqrt(_segsum(kk * kk, gm) + 1e-12)
    k2 = k * (1.0 + (a - 1.0) * kaw[...])
    r_o[...] = r
    w_o[...] = decay
    k_o[...] = k2
    v_o[...] = v
    nkk_o[...] = -kk
    kka_o[...] = kk * a
    gate_o[...] = gate
    bon_o[...] = _segsum(r * k2 * rkw[...], gm) * v


def _rwprep(p2, starts, params, gm, *, tm, seq_len, rcol, lcol):
    rows = p2.shape[0]
    wid = gm.shape[0]
    lw = 2 * LANES
    has_start = starts is not None
    row_spec = lambda w, cb: pl.BlockSpec((tm, w), lambda i: (i, cb))
    in_specs = [row_spec(wid, rcol), row_spec(wid, rcol + 1), row_spec(wid, rcol + 2), row_spec(lw, lcol)]
    args = [p2, p2, p2, p2]
    if has_start:
        in_specs += [row_spec(wid, 0), row_spec(wid, 0), row_spec(wid, 0), row_spec(lw, 0)]
        args += list(starts)
    for prm in params:
        in_specs.append(pl.BlockSpec(prm.shape, lambda i: (0, 0)))
    in_specs.append(pl.BlockSpec(gm.shape, lambda i: (0, 0)))
    args += list(params) + [gm]
    return pl.pallas_call(
        functools.partial(_rwprep_kernel, seq_len=seq_len, has_start=has_start),
        grid=(rows // tm,),
        in_specs=in_specs,
        out_specs=[pl.BlockSpec((tm, wid), lambda i: (i, 0))] * 8,
        out_shape=[jax.ShapeDtypeStruct((rows, wid), F32)] * 8,
        scratch_shapes=[pltpu.VMEM((8, wid), F32)] * 3 + [pltpu.VMEM((8, lw), F32)],
        compiler_params=_cparams("arbitrary"),
        name="rwprep",
    )(*args)


def _scan_kernel(r_ref, w_ref, k_ref, v_ref, a_ref, b_ref, s0_ref, y_ref, sT_ref, s_ref, *, tb):
    tblk = pl.program_id(1)
    dk = s_ref.shape[0]

    @pl.when(tblk == 0)
    def _():
        s_ref[...] = s0_ref[...]

    def step(t, carry):
        sa = s_ref[0] * a_ref[t, pl.ds(0, 1), :]
        for kx in range(1, dk):
            sa = sa + s_ref[kx] * a_ref[t, pl.ds(kx, 1), :]
        vt = v_ref[t]
        y = None
        for kx in range(dk):
            sk = (s_ref[kx] * w_ref[t, pl.ds(kx, 1), :] + sa * b_ref[t, pl.ds(kx, 1), :]
                  + vt * k_ref[t, pl.ds(kx, 1), :])
            s_ref[kx] = sk
            yk = sk * r_ref[t, pl.ds(kx, 1), :]
            y = yk if y is None else y + yk
        y_ref[t] = y
        return carry

    lax.fori_loop(0, tb, step, 0)

    @pl.when(tblk == pl.num_programs(1) - 1)
    def _():
        sT_ref[...] = s_ref[...]


def _scan(seqs, s0, *, tb, t_off, n_steps):
    npairs = s0.shape[-1]
    dk, dv = s0.shape[0], s0.shape[1]
    seq_spec = pl.BlockSpec((tb, dk, LANES), lambda g, t: (t + t_off // tb, 0, g))
    st_spec = pl.BlockSpec((dk, dv, LANES), lambda g, t: (0, 0, g))
    return pl.pallas_call(
        functools.partial(_scan_kernel, tb=tb),
        grid=(npairs // LANES, n_steps // tb),
        in_specs=[seq_spec] * 6 + [st_spec],
        out_specs=[pl.BlockSpec((tb, dv, LANES), lambda g, t: (t, 0, g)), st_spec],
        out_shape=[jax.ShapeDtypeStruct((n_steps, dv, npairs), F32), jax.ShapeDtypeStruct(s0.shape, F32)],
        scratch_shapes=[pltpu.VMEM((dk, dv, LANES), F32)],
        compiler_params=_cparams("arbitrary", "arbitrary"),
        name="wkvscan",
    )(*seqs, s0)


def _mix_kernel(x_ref, o_ref, og_ref, y_ref, gate_ref, bon_ref, on_ref, lnw_ref, lnb_ref, n2_ref,
                wo_ref, g_ref, h_ref, hn_ref):
    gm = g_ref[...]
    half = o_ref.shape[1]
    o = o_ref[...]
    fox = (o * lax.rsqrt(_segsum(o * o, gm) * (1.0 / HEAD_DIM) + RMS_EPS) * on_ref[...]
           * jax.nn.sigmoid(og_ref[...]))
    y = y_ref[...]
    d = y - _segsum(y, gm) * (1.0 / HEAD_DIM)
    var = _segsum(d * d, gm) * (1.0 / HEAD_DIM)
    rw = (d * lax.rsqrt(var + GN_EPS) * lnw_ref[...] + lnb_ref[...] + bon_ref[...]) * gate_ref[...]
    h = (x_ref[...] + _dot(fox.astype(BF16), wo_ref[0:half, :]) + _dot(rw.astype(BF16), wo_ref[half:, :]))
    h_ref[...] = h
    ms = jnp.mean(h * h, axis=-1, keepdims=True)
    hn_ref[...] = h * lax.rsqrt(ms + RMS_EPS) * n2_ref[...]


def _mix(x, o, og_src, y, gate_src, bon_src, vecs, wo, gm, *, n_blocks, og_map, rw_map):
    tm = SEQ_BLOCK
    d = x.shape[1]
    half = d // 2
    blk = lambda w, f: pl.BlockSpec((tm, w), f)
    const = lambda a: pl.BlockSpec(a.shape, lambda i: (0, 0))
    return pl.pallas_call(
        _mix_kernel,
        grid=(n_blocks,),
        in_specs=[blk(d, lambda i: (i, 0)), blk(half, lambda i: (i, 0)), blk(half, og_map),
                  blk(half, lambda i: (i, 0)), blk(half, rw_map), blk(half, rw_map)]
        + [const(v) for v in vecs] + [const(wo), const(gm)],
        out_specs=[blk(d, lambda i: (i, 0))] * 2,
        out_shape=[jax.ShapeDtypeStruct((n_blocks * tm, d), F32)] * 2,
        compiler_params=_cparams("arbitrary"),
        name="mix",
    )(x, o, og_src, y, gate_src, bon_src, *vecs, wo, gm)


def _extract_max(s_ref, t, out_v, out_i, idx_ref=None):
    n_rows = s_ref.shape[0]
    pos = lax.broadcasted_iota(jnp.int32, s_ref.shape, 0)
    s = s_ref[...]
    m = jnp.max(s, axis=0, keepdims=True)
    am = jnp.min(jnp.where(s == m, pos, n_rows), axis=0, keepdims=True)
    hit = pos == am
    out_v[pl.ds(t, 1), :] = m
    if idx_ref is None:
        out_i[pl.ds(t, 1), :] = am
    else:
        out_i[pl.ds(t, 1), :] = jnp.max(jnp.where(hit, idx_ref[...], -1), axis=0, keepdims=True)
    s_ref[...] = jnp.where(hit, -jnp.inf, s)


_CAND_KEEP = [PEER_TOPK // (i + 1) for i in range(PEER_TOPK)]
_CAND_ROWS = [-(-n // 8) * 8 for n in _CAND_KEEP]
_CAND_OFF = [sum(_CAND_ROWS[:i]) for i in range(PEER_TOPK)]
_N_CAND = sum(_CAND_ROWS)


def _peersel_kernel(hn_ref, wq_ref, keys_ref, eidx_ref, gate_ref, s_ref, sv_ref, si_ref, cand_ref, cidx_ref,
                    fv_ref, fi_ref):
    q = _dot(hn_ref[...].astype(BF16), wq_ref[...])
    nprob = 2 * PEER_HEADS
    for h in range(PEER_HEADS):
        qh = q[:, h * LANES:(h + 1) * LANES].astype(BF16)
        for half in range(2):
            s_ref[2 * h + half] = _dot_nt(keys_ref[h, half], qh)
    for t in range(PEER_TOPK):
        for n in range(nprob):
            _extract_max(s_ref.at[n], t, sv_ref.at[n], si_ref.at[n])
    for h in range(PEER_HEADS):
        for i in range(PEER_TOPK):
            nr = _CAND_ROWS[i]
            rows = slice(_CAND_OFF[i], _CAND_OFF[i] + nr)
            keep = lax.broadcasted_iota(jnp.int32, (nr, sv_ref.shape[2]), 0) < _CAND_KEEP[i]
            cand = sv_ref[2 * h, pl.ds(i, 1), :] + sv_ref[2 * h + 1, 0:nr, :]
            cand_ref[h, rows, :] = jnp.where(keep, cand, -jnp.inf)
            cidx_ref[h, rows, :] = si_ref[2 * h, pl.ds(i, 1), :] * PEER_N_KEYS + si_ref[2 * h + 1, 0:nr, :]
    for t in range(PEER_TOPK):
        for h in range(PEER_HEADS):
            _extract_max(cand_ref.at[h], t, fv_ref.at[h], fi_ref.at[h], idx_ref=cidx_ref.at[h])
    for h in range(PEER_HEADS):
        fv = fv_ref[h]
        e = jnp.exp(fv - fv[0:1, :])
        rows = slice(h * PEER_TOPK, (h + 1) * PEER_TOPK)
        gate_ref[rows, :] = e / jnp.sum(e, axis=0, keepdims=True)
        eidx_ref[rows, :] = fi_ref[h]


def _peer_select(hn, wq, keys):
    n, d = hn.shape
    tm = SEQ_BLOCK
    nsel = PEER_HEADS * PEER_TOPK
    nprob = 2 * PEER_HEADS
    return pl.pallas_call(
        _peersel_kernel,
        grid=(n // tm,),
        in_specs=[pl.BlockSpec((tm, d), lambda i: (i, 0)),
                  pl.BlockSpec(wq.shape, lambda i: (0, 0)),
                  pl.BlockSpec(keys.shape, lambda i: (0, 0, 0, 0))],
        out_specs=[pl.BlockSpec((nsel, tm), lambda i: (0, i))] * 2,
        out_shape=[jax.ShapeDtypeStruct((nsel, n), jnp.int32), jax.ShapeDtypeStruct((nsel, n), F32)],
        scratch_shapes=[pltpu.VMEM((nprob, PEER_N_KEYS, tm), F32),
                        pltpu.VMEM((nprob, PEER_TOPK, tm), F32), pltpu.VMEM((nprob, PEER_TOPK, tm), jnp.int32),
                        pltpu.VMEM((PEER_HEADS, _N_CAND, tm), F32), pltpu.VMEM((PEER_HEADS, _N_CAND, tm), jnp.int32),
                        pltpu.VMEM((PEER_HEADS, PEER_TOPK, tm), F32),
                        pltpu.VMEM((PEER_HEADS, PEER_TOPK, tm), jnp.int32)],
        compiler_params=_cparams("arbitrary"),
        name="peersel",
    )(hn, wq, keys)


def _peer_kernel(eidx_hbm, table_hbm, x_ref, gate_ref, hres_ref, out_ref, idx_smem, buf, part_ref, sem_idx, sems,
                 *, tb, nsel, rows_per_expert, nslot):
    i = pl.program_id(0)
    rpe = rows_per_expert
    nchunk = rpe // 2
    slot_rows = nsel * rpe
    cp = pltpu.make_async_copy(eidx_hbm.at[pl.ds(pl.multiple_of(i * (tb * nsel), tb * nsel), tb * nsel)],
                               idx_smem, sem_idx)
    cp.start()
    cp.wait()

    grp = 8
    ngrp = nsel // grp

    def issue(t, slot):
        def body(g, carry):
            for j in range(grp):
                kx = g * grp + j
                e = idx_smem[t * nsel + kx]
                pltpu.make_async_copy(table_hbm.at[pl.ds(pl.multiple_of(e * rpe, rpe), rpe), :],
                                      buf.at[slot, pl.ds(kx * rpe, rpe), :],
                                      sems.at[slot]).start(priority=j % 2)
            return carry

        for g in range(ngrp):
            body(g, 0)

    def wait(slot):
        pltpu.make_async_copy(table_hbm.at[pl.ds(0, slot_rows), :], buf.at[slot], sems.at[slot]).wait()

    lane = lax.broadcasted_iota(jnp.int32, gate_ref.shape, 1)
    sub8 = lax.broadcasted_iota(jnp.int32, (8, LANES), 0)

    def fold(a, b, shift):
        low = (sub8 & shift) == 0
        return jnp.where(low, a + pltpu.roll(a, 8 - shift, axis=0), b + pltpu.roll(b, shift, axis=0))

    def sublane_sums(ts):
        ys = [fold(ts[j], ts[j + 4], 4) for j in range(4)]
        zs = [fold(ys[j], ys[j + 2], 2) for j in range(2)]
        return fold(zs[0], zs[1], 1)

    def compute(t, slot):
        xt = x_ref[t]

        def hidden(g, carry):
            base = g * (grp * rpe)
            ts = []
            for j in range(grp):
                p = buf[slot, pl.ds(base + j * rpe, nchunk), :].astype(F32) * xt
                acc = p[0:8]
                for s in range(8, nchunk, 8):
                    acc = acc + p[s:s + 8]
                ts.append(acc)
            part_ref[pl.ds(g * grp, grp), :] = sublane_sums(ts)
            return carry

        for g in range(ngrp):
            hidden(g, 0)
        hid = jnp.sum(part_ref[...], axis=-1, keepdims=True)
        gate = jnp.sum(jnp.where(lane == t, gate_ref[...], 0.0), axis=-1, keepdims=True)
        act = gate * (hid * (lax.erf(hid / np.sqrt(2.0).astype(np.float32)) + 1.0) / 2.0)
        part_ref[...] = jnp.broadcast_to(act, (nsel, LANES))

        def combine(g, accs):
            base = g * (grp * rpe)
            a8 = part_ref[pl.ds(g * grp, grp), :]
            accs = list(accs)
            for j in range(grp):
                vrow = buf[slot, pl.ds(base + j * rpe + nchunk, nchunk), :].astype(F32)
                accs[j % 4] = accs[j % 4] + vrow * a8[j:j + 1, :]
            return tuple(accs)

        zero = jnp.zeros((nchunk, LANES), F32)
        accs = (zero, zero, zero, zero)
        for g in range(ngrp):
            accs = combine(g, accs)
        out_ref[t] = hres_ref[t] + ((accs[0] + accs[1]) + (accs[2] + accs[3]))

    for t in range(nslot - 1):
        issue(t, t)

    def body(j, carry):
        base = j * nslot
        for sub in range(nslot):
            @pl.when(base + sub + nslot - 1 < tb)
            def _():
                issue(base + sub + nslot - 1, (sub + nslot - 1) % nslot)

            wait(sub)
            compute(base + sub, sub)
        return carry

    lax.fori_loop(0, tb // nslot, body, 0)


def _peer_combine(eidx_flat, table, hn, gate_t, hres, *, nsel, rows_per_expert):
    n, d = hn.shape
    tb = SEQ_BLOCK
    nslot = 4
    nchunk = d // LANES
    tok = pl.BlockSpec((tb, nchunk, LANES), lambda i: (i, 0, 0))
    out = pl.pallas_call(
        functools.partial(_peer_kernel, tb=tb, nsel=nsel, rows_per_expert=rows_per_expert, nslot=nslot),
        grid=(n // tb,),
        in_specs=[pl.BlockSpec(memory_space=pl.ANY),
                  pl.BlockSpec(memory_space=pl.ANY),
                  tok,
                  pl.BlockSpec((nsel, tb), lambda i: (0, i)),
                  tok],
        out_specs=tok,
        out_shape=jax.ShapeDtypeStruct((n, nchunk, LANES), F32),
        scratch_shapes=[pltpu.SMEM((tb * nsel,), jnp.int32),
                        pltpu.VMEM((nslot, nsel * rows_per_expert, LANES), table.dtype),
                        pltpu.VMEM((nsel, LANES), F32),
                        pltpu.SemaphoreType.DMA(()),
                        pltpu.SemaphoreType.DMA((nslot,))],
        compiler_params=_cparams("arbitrary"),
        name="peer",
    )(eidx_flat, table, hn.reshape(n, nchunk, LANES), gate_t, hres.reshape(n, nchunk, LANES))
    return out.reshape(n, d)


def _to_pairs(a, b, l):
    nh = a.shape[1] // HEAD_DIM
    a = a.reshape(b, l, nh, HEAD_DIM).transpose(1, 3, 0, 2).reshape(l, HEAD_DIM, b * nh)
    pad = (-a.shape[-1]) % LANES
    return jnp.pad(a, ((0, 0), (0, 0), (0, pad))) if pad else a


def kernel(x_prompt, x_sample, cache_k, cache_v, cache_logf, state_wkv, state_shift, page_table, meta_tokens,
           norm1_w, w_in, fox_bf, fox_qn, fox_kn, fox_on, rw_mu, rw_w0, rw_w2, rw_a0, rw_a2, rw_g2, rw_kk,
           rw_ka, rw_rk, rw_ln_w, rw_ln_b, w_o, norm2_w, peer_wq, peer_keys, peer_u, peer_v):
    depth = w_in.shape[0]
    assert depth == 1, "single-layer step"
    bsz, seq, d = x_prompt.shape
    bd, tdec, _ = x_sample.shape
    half = d // 2
    nh = half // HEAD_DIM
    nhp = half // LANES
    lp = FRONT_PAD + N_META + seq
    nq = seq // SEQ_BLOCK
    lora_w = 2 * LANES
    assert seq % SEQ_BLOCK == 0 and rw_w2.shape[1] + rw_a2.shape[1] == LANES and rw_g2.shape[1] == LANES

    tn = lora_w
    w = w_in[0]
    fox_cols = 4 * half + nh
    wq_, wk_, wv_ = w[:, 0:half], w[:, half:2 * half], w[:, 2 * half:3 * half]
    wfl, wog = w[:, 3 * half:3 * half + nh], w[:, 3 * half + nh:fox_cols]
    wrw = w[:, fox_cols:]
    w_all = jnp.concatenate([wq_, wk_, wv_, wog, wrw, jnp.pad(wfl, ((0, 0), (0, tn - nh)))], axis=1).astype(BF16)
    ncol = w_all.shape[1]
    nblk = ncol // tn
    n_norm = 2 * half // tn
    j_fl = nblk - 1
    rep = tn // HEAD_DIM
    cvec = jnp.zeros((nblk, tn), F32)
    cvec = cvec.at[0:n_norm // 2].set(jnp.tile(fox_qn[0] * (HEAD_DIM ** -0.5), rep))
    cvec = cvec.at[n_norm // 2:n_norm].set(jnp.tile(fox_kn[0], rep))
    cvec = cvec.at[j_fl, 0:nh].set(fox_bf[0]).reshape(nblk, 1, tn)
    g_tn = _head_ones(tn)
    g_half = _head_ones(half)
    nw1 = norm1_w[0].reshape(1, d)
    col_k, col_v, col_og, col_rw = half, 2 * half, 3 * half, 4 * half
    col_lora = col_rw + 3 * half
    col_fl = col_lora + lora_w

    mu = rw_mu[0]
    row = lambda a: a.reshape(1, -1)
    zpad = jnp.zeros((LANES - rw_w2.shape[1], half), F32)
    rw_params = [row(mu[0:half]), row(mu[half:2 * half]), row(mu[2 * half:3 * half]), row(mu[3 * half:]),
                 row(rw_w0[0]), jnp.concatenate([rw_w2[0], zpad], 0).astype(BF16),
                 row(rw_a0[0]), jnp.concatenate([zpad, rw_a2[0]], 0).astype(BF16),
                 rw_g2[0].astype(BF16), row(rw_kk[0]), row(rw_ka[0]), row(rw_rk[0])]
    mix_vecs = [row(fox_on[0]), row(rw_ln_w[0]), row(rw_ln_b[0]), row(norm2_w[0])]
    wo = w_o[0].astype(BF16)
    wq_peer = peer_wq[0].astype(BF16)
    khalf = peer_keys.shape[-1]
    keys = peer_keys[0].astype(BF16)
    keys = jnp.stack([jnp.pad(keys[:, 0], ((0, 0), (0, 0), (0, LANES - khalf))),
                      jnp.pad(keys[:, 1], ((0, 0), (0, 0), (khalf, LANES - 2 * khalf))) ], axis=1)
    rpe = 2 * d // LANES
    table = jnp.concatenate([peer_u[0], peer_v[0]], axis=1).astype(BF16).reshape(-1, LANES)
    nsel = PEER_HEADS * PEER_TOPK

    def peer(hn, hres):
        eidx_t, gate_t = _peer_select(hn, wq_peer, keys)
        return _peer_combine(eidx_t.T.reshape(-1), table, hn, gate_t, hres, nsel=nsel, rows_per_expert=rpe)

    hp = jnp.concatenate([jnp.zeros((bsz, FRONT_PAD, d), F32),
                          jnp.broadcast_to(meta_tokens.astype(F32)[None], (bsz, N_META, d)),
                          x_prompt.astype(F32)], axis=1).reshape(bsz * lp, d)
    tm_p = 1024 if (bsz * lp) % 1024 == 0 else SEQ_BLOCK
    pp = _inproj(hp, nw1, w_all, cvec, g_tn, tm=tm_p, tn=tn, n_norm=n_norm, j_fl=j_fl)
    pp3 = pp.reshape(bsz, lp, ncol)
    ccol, crow = _fcum(pp3, col_fl // LANES)
    ccol = ccol[:, :, 0:nh].reshape(bsz, lp, nhp, 2).transpose(0, 2, 1, 3)
    crow = crow[:, 0:nh, :].reshape(bsz, nhp, 2, lp)
    o_p = _attention(pp3, ccol, crow, nq=nq, kcol=col_k // LANES, vcol=col_v // LANES)
    o_p = o_p.reshape(bsz * seq, half)

    rw_p = _rwprep(pp, None, rw_params, g_half, tm=SEQ_BLOCK, seq_len=lp,
                   rcol=col_rw // half, lcol=col_lora // lora_w)
    seqs_p = [_to_pairs(a, bsz, lp) for a in rw_p[0:6]]
    npair_p = seqs_p[0].shape[-1]
    tb_p = N_META
    y_p, s_p = _scan(seqs_p, jnp.zeros((HEAD_DIM, HEAD_DIM, npair_p), F32),
                     tb=tb_p, t_off=FRONT_PAD, n_steps=N_META + seq)
    y_p = y_p[N_META:, :, 0:bsz * nh].reshape(seq, HEAD_DIM, bsz, nh).transpose(2, 0, 3, 1).reshape(bsz * seq, half)
    wkv_p = s_p[:, :, 0:bsz * nh].transpose(2, 1, 0).reshape(1, bsz, nh, HEAD_DIM, HEAD_DIM)

    blocks_per_seq = lp // SEQ_BLOCK
    in_rows = lambda i: (i // nq) * blocks_per_seq + 1 + i % nq
    h_p, hn_p = _mix(x_prompt.reshape(bsz * seq, d).astype(F32), o_p, pp, y_p, rw_p[6], rw_p[7], mix_vecs, wo,
                     g_half, n_blocks=bsz * nq,
                     og_map=lambda i: (in_rows(i), col_og // half), rw_map=lambda i: (in_rows(i), 0))
    y_prompt = peer(hn_p, h_p).reshape(bsz, seq, d)

    pr = pp3[:, FRONT_PAD:]
    k_prompt = pr[:, :, col_k:col_k + half].reshape(1, bsz, N_META + seq, nh, HEAD_DIM)
    v_prompt = pr[:, :, col_v:col_v + half].reshape(1, bsz, N_META + seq, nh, HEAD_DIM)
    logf_prompt = pr[:, :, col_fl:col_fl + nh].reshape(1, bsz, N_META + seq, nh)
    shift_prompt = pp3[:, -1, col_rw:col_lora + lora_w].reshape(1, bsz, -1)

    nrow_d = bd * tdec
    rows_d = -(-nrow_d // SEQ_BLOCK) * SEQ_BLOCK
    xs = x_sample.astype(F32).reshape(nrow_d, d)
    xs_pad = jnp.pad(xs, ((0, rows_d - nrow_d), (0, 0)))
    pd_ = _inproj(xs_pad, nw1, w_all, cvec, g_tn, tm=SEQ_BLOCK, tn=tn, n_norm=n_norm, j_fl=j_fl)
    pd3 = pd_[0:nrow_d].reshape(bd, tdec, ncol)
    tpad = SEQ_BLOCK // nh
    n_pool, page_size = cache_k.shape[1], cache_k.shape[2]
    pages = lambda c: c[0].astype(F32).transpose(0, 2, 3, 1).reshape(n_pool, half, page_size)
    q4 = pd3[:, :, 0:half].reshape(bd, tdec, nh, HEAD_DIM)
    q_d = jnp.eye(nh, dtype=F32)[None, :, None, :, None] * q4[:, None]
    q_d = jnp.pad(q_d, ((0, 0), (0, 0), (0, tpad - tdec), (0, 0), (0, 0))).reshape(bd, SEQ_BLOCK, half).astype(BF16)
    padk = ((0, 0), (0, 0), (0, page_size - tdec))
    knew = jnp.pad(pd3[:, :, col_k:col_k + half].transpose(0, 2, 1), padk)
    vnew = jnp.pad(pd3[:, :, col_v:col_v + half].transpose(0, 2, 1), padk)
    lnew = jnp.pad(pd3[:, :, col_fl:col_fl + nh].transpose(0, 2, 1), padk)
    bnew, tnew = _page_suffix(lnew)
    suf, tot = _page_suffix(cache_logf[0].astype(F32).transpose(0, 2, 1))
    o_d = _decode_attention(page_table, q_d, knew, vnew, bnew, tnew, pages(cache_k), pages(cache_v), suf, tot,
                            n_new=tdec, n_heads=nh)
    o_d = jnp.pad(o_d[:, 0:tdec].reshape(nrow_d, half), ((0, rows_d - nrow_d), (0, 0)))

    sh = state_shift[0].astype(F32)
    st = jnp.pad(jnp.repeat(sh, tdec, axis=0), ((0, rows_d - nrow_d), (0, 0)))
    starts = [st[:, 0:half], st[:, half:2 * half], st[:, 2 * half:3 * half], st[:, 3 * half:]]
    rw_d = _rwprep(pd_, starts, rw_params, g_half, tm=SEQ_BLOCK, seq_len=tdec,
                   rcol=col_rw // half, lcol=col_lora // lora_w)
    seqs_d = [_to_pairs(a[0:nrow_d], bd, tdec) for a in rw_d[0:6]]
    s0_d = state_wkv[0].astype(F32).reshape(bd * nh, HEAD_DIM, HEAD_DIM).transpose(2, 1, 0)
    pad_d = (-s0_d.shape[-1]) % LANES
    if pad_d:
        s0_d = jnp.pad(s0_d, ((0, 0), (0, 0), (0, pad_d)))
    y_d, s_d = _scan(seqs_d, s0_d, tb=tdec, t_off=0, n_steps=tdec)
    y_d = y_d[:, :, 0:bd * nh].reshape(tdec, HEAD_DIM, bd, nh).transpose(2, 0, 3, 1).reshape(nrow_d, half)
    y_d = jnp.pad(y_d, ((0, rows_d - nrow_d), (0, 0)))
    wkv_d = s_d[:, :, 0:bd * nh].transpose(2, 1, 0).reshape(1, bd, nh, HEAD_DIM, HEAD_DIM)
    h_d, hn_d = _mix(xs_pad, o_d, pd_, y_d, rw_d[6], rw_d[7], mix_vecs, wo, g_half, n_blocks=rows_d // SEQ_BLOCK,
                     og_map=lambda i: (i, col_og // half), rw_map=lambda i: (i, 0))
    y_sample = peer(hn_d, h_d)[0:nrow_d].reshape(bd, tdec, d)

    k_sample = pd3[:, :, col_k:col_k + half].reshape(1, bd, tdec, nh, HEAD_DIM)
    v_sample = pd3[:, :, col_v:col_v + half].reshape(1, bd, tdec, nh, HEAD_DIM)
    logf_sample = pd3[:, :, col_fl:col_fl + nh].reshape(1, bd, tdec, nh)
    shift_sample = pd3[:, -1, col_rw:col_lora + lora_w].reshape(1, bd, -1)

    return (y_prompt, y_sample, k_prompt, v_prompt, logf_prompt, wkv_p, shift_prompt,
            k_sample, v_sample, logf_sample, wkv_d, shift_sample)
```

```python
import functools

import numpy as np
import jax
import jax.numpy as jnp
from jax import lax
from jax.experimental import pallas as pl
from jax.experimental.pallas import tpu as pltpu

F32 = jnp.float32
BF16 = jnp.bfloat16

LANES = 128
HEAD_DIM = 64
N_META = 16
PEER_HEADS = 8
PEER_N_KEYS = 128
PEER_TOPK = 16
RMS_EPS = 1e-6
GN_EPS = HEAD_DIM * 1e-5
NEG_BIG = -1e30
SEQ_BLOCK = 128
FRONT_PAD = SEQ_BLOCK - N_META
VMEM_LIMIT = 48 * 1024 * 1024
PAGE_GROUP = 4


def _cparams(*sem):
    return pltpu.CompilerParams(dimension_semantics=sem, vmem_limit_bytes=VMEM_LIMIT)


def _dot(a, b):
    return jnp.dot(a, b, preferred_element_type=F32)


def _dot_nt(a, b):
    return lax.dot_general(a, b, (((1,), (1,)), ((), ())), preferred_element_type=F32)


def _split2(x):
    hi = x.astype(BF16)
    lo = (x - hi.astype(F32)).astype(BF16)
    return hi, lo


def _split3(x):
    hi = x.astype(BF16)
    r = x - hi.astype(F32)
    mid = r.astype(BF16)
    lo = (r - mid.astype(F32)).astype(BF16)
    return hi, mid, lo


def _segsum(x, g):
    hi, lo = _split2(x)
    return _dot(hi, g) + _dot(lo, g)


def _log_sigmoid(z):
    return jnp.minimum(z, 0.0) - jnp.log1p(jnp.exp(-jnp.abs(z)))


def _head_ones(n):
    i = np.arange(n) // HEAD_DIM
    return jnp.asarray(i[:, None] == i[None, :], dtype=BF16)


def _inproj_kernel(x_ref, nw_ref, w_ref, cvec_ref, g_ref, o_ref, xn_ref, *, n_norm, j_fl):
    j = pl.program_id(1)

    @pl.when(j == 0)
    def _():
        x = x_ref[...]
        ms = jnp.mean(x * x, axis=-1, keepdims=True)
        xn_ref[...] = (x * lax.rsqrt(ms + RMS_EPS) * nw_ref[...]).astype(BF16)

    acc = _dot(xn_ref[...], w_ref[...])
    vec = cvec_ref[0]

    @pl.when(j < n_norm)
    def _():
        ss = _segsum(acc * acc, g_ref[...])
        o_ref[...] = acc * lax.rsqrt(ss * (1.0 / HEAD_DIM) + RMS_EPS) * vec

    @pl.when(j == j_fl)
    def _():
        o_ref[...] = _log_sigmoid(acc + vec)

    @pl.when(jnp.logical_and(j >= n_norm, j != j_fl))
    def _():
        o_ref[...] = acc


def _inproj(x, nw, w, cvec, g, *, tm, tn, n_norm, j_fl):
    rows, d = x.shape
    cols = w.shape[1]
    return pl.pallas_call(
        functools.partial(_inproj_kernel, n_norm=n_norm, j_fl=j_fl),
        grid=(rows // tm, cols // tn),
        in_specs=[
            pl.BlockSpec((tm, d), lambda i, j: (i, 0)),
            pl.BlockSpec((1, d), lambda i, j: (0, 0)),
            pl.BlockSpec((d, tn), lambda i, j: (0, j)),
            pl.BlockSpec((1, 1, tn), lambda i, j: (j, 0, 0)),
            pl.BlockSpec((tn, tn), lambda i, j: (0, 0)),
        ],
        out_specs=pl.BlockSpec((tm, tn), lambda i, j: (i, j)),
        out_shape=jax.ShapeDtypeStruct((rows, cols), F32),
        scratch_shapes=[pltpu.VMEM((tm, d), BF16)],
        compiler_params=_cparams("arbitrary", "arbitrary"),
        name="inproj",
    )(x, nw, w, cvec, g)


def _fcum_kernel(lf_ref, ccol_ref, crow_ref, *, nblk):
    r = lax.broadcasted_iota(jnp.int32, (SEQ_BLOCK, SEQ_BLOCK), 0)
    c = lax.broadcasted_iota(jnp.int32, (SEQ_BLOCK, SEQ_BLOCK), 1)
    upper = (c > r).astype(BF16)
    lower = (r > c).astype(BF16)

    def body(n, carry):
        ccar, rcar = carry
        i = nblk - 1 - n
        off = pl.multiple_of(i * SEQ_BLOCK, SEQ_BLOCK)
        x = lf_ref[0, pl.ds(off, SEQ_BLOCK), :]
        h, m, l = _split3(x)
        suf = _dot(upper, h) + _dot(upper, m) + _dot(upper, l)
        ccol_ref[0, pl.ds(off, SEQ_BLOCK), :] = -(suf + ccar)
        xt = x.T
        ht, mt, lt = _split3(xt)
        suft = _dot(ht, lower) + _dot(mt, lower) + _dot(lt, lower)
        crow_ref[0, :, pl.ds(off, SEQ_BLOCK)] = -(suft + rcar)
        return (ccar + suf[0:1, :] + x[0:1, :], rcar + suft[:, 0:1] + xt[:, 0:1])

    lax.fori_loop(0, nblk, body, (jnp.zeros((1, LANES), F32), jnp.zeros((LANES, 1), F32)))


def _fcum(p3, col_block):
    b, lp, _ = p3.shape
    return pl.pallas_call(
        functools.partial(_fcum_kernel, nblk=lp // SEQ_BLOCK),
        grid=(b,),
        in_specs=[pl.BlockSpec((1, lp, LANES), lambda i: (i, 0, col_block))],
        out_specs=[pl.BlockSpec((1, lp, LANES), lambda i: (i, 0, 0)),
                   pl.BlockSpec((1, LANES, lp), lambda i: (i, 0, 0))],
        out_shape=[jax.ShapeDtypeStruct((b, lp, LANES), F32), jax.ShapeDtypeStruct((b, LANES, lp), F32)],
        compiler_params=_cparams("arbitrary"),
        name="fcum",
    )(p3)


def _attn_kernel(q_ref, k_ref, v_ref, ccol_ref, crow_ref, o_ref, *, kc):
    qi = pl.program_id(2)
    tq = SEQ_BLOCK
    lp = k_ref.shape[1]
    lo = lax.broadcasted_iota(jnp.int32, (tq, LANES), 1) < HEAD_DIM
    q = q_ref[0]
    qh = (jnp.where(lo, q, 0.0).astype(BF16), jnp.where(lo, 0.0, q).astype(BF16))
    cq = (ccol_ref[0, 0, :, 0:1], ccol_ref[0, 0, :, 1:2])
    qpos = (qi + 1) * tq + lax.broadcasted_iota(jnp.int32, (tq, kc), 0)
    koff = lax.broadcasted_iota(jnp.int32, (tq, kc), 1)

    def chunk(c, carry):
        first = c * kc
        start = pl.multiple_of(jnp.minimum(first, lp - kc), SEQ_BLOCK)
        kpos = start + koff
        mask = jnp.logical_and(kpos >= jnp.maximum(first, FRONT_PAD), kpos <= qpos)
        kb = k_ref[0, pl.ds(start, kc), :].astype(BF16)
        vb = v_ref[0, pl.ds(start, kc), :].astype(BF16)
        out = []
        for h in range(2):
            m, l, acc = carry[h]
            s = _dot_nt(qh[h], kb) + (cq[h] - crow_ref[0, 0, pl.ds(h, 1), pl.ds(start, kc)])
            s = jnp.where(mask, s, NEG_BIG)
            m_new = jnp.maximum(m, jnp.max(s, axis=-1, keepdims=True))
            p = jnp.exp(s - m_new)
            corr = jnp.exp(m - m_new)
            out.append((m_new, l * corr + jnp.sum(p, axis=-1, keepdims=True),
                        acc * corr + _dot(p.astype(BF16), vb)))
        return tuple(out)

    init = tuple((jnp.full((tq, 1), NEG_BIG, F32), jnp.zeros((tq, 1), F32), jnp.zeros((tq, LANES), F32))
                 for _ in range(2))
    carry = lax.fori_loop(0, ((qi + 2) * tq + kc - 1) // kc, chunk, init)
    o_ref[0] = jnp.where(lo, carry[0][2] / carry[0][1], carry[1][2] / carry[1][1])


def _attention(p3, ccol, crow, *, nq, kcol, vcol):
    b, lp, _ = p3.shape
    nhp = ccol.shape[1]
    kc = max(c for c in (4 * SEQ_BLOCK, 2 * SEQ_BLOCK, SEQ_BLOCK) if c <= lp)
    return pl.pallas_call(
        functools.partial(_attn_kernel, kc=kc),
        grid=(b, nhp, nq),
        in_specs=[
            pl.BlockSpec((1, SEQ_BLOCK, LANES), lambda i, h, q: (i, q + 1, h)),
            pl.BlockSpec((1, lp, LANES), lambda i, h, q: (i, 0, kcol + h)),
            pl.BlockSpec((1, lp, LANES), lambda i, h, q: (i, 0, vcol + h)),
            pl.BlockSpec((1, 1, SEQ_BLOCK, 2), lambda i, h, q: (i, h, q + 1, 0)),
            pl.BlockSpec((1, 1, 2, lp), lambda i, h, q: (i, h, 0, 0)),
        ],
        out_specs=pl.BlockSpec((1, SEQ_BLOCK, LANES), lambda i, h, q: (i, q, h)),
        out_shape=jax.ShapeDtypeStruct((b, nq * SEQ_BLOCK, nhp * LANES), F32),
        compiler_params=_cparams("arbitrary", "arbitrary", "arbitrary"),
        name="attn",
    )(p3, p3, p3, ccol, crow)


def _pagesuf_kernel(lf_ref, suf_ref, tot_ref):
    pb, nh, n = lf_ref.shape
    r = lax.broadcasted_iota(jnp.int32, (n, n), 0)
    c = lax.broadcasted_iota(jnp.int32, (n, n), 1)
    lower = (r > c).astype(BF16)
    x = lf_ref[...].reshape(pb * nh, n)
    a, b_, c_ = _split3(x)
    suf = _dot(a, lower) + _dot(b_, lower) + _dot(c_, lower)
    suf_ref[...] = suf.reshape(pb, nh, n)
    tot_ref[...] = jnp.broadcast_to(suf[:, 0:1] + x[:, 0:1], x.shape).reshape(pb, nh, n)


def _page_suffix(lf):
    n, nh, ps = lf.shape
    pb = max(c for c in (32, 16, 8, 4, 2, 1) if n % c == 0)
    spec = pl.BlockSpec((pb, nh, ps), lambda i: (i, 0, 0))
    return pl.pallas_call(
        _pagesuf_kernel,
        grid=(n // pb,),
        in_specs=[spec],
        out_specs=[spec, spec],
        out_shape=[jax.ShapeDtypeStruct(lf.shape, F32)] * 2,
        compiler_params=_cparams("arbitrary"),
        name="pagesuf",
    )(lf)


def _decattn_kernel(pt_ref, q_ref, knew_ref, vnew_ref, bnew_ref, tnew_ref, *refs, n_new, n_heads, group):
    page_refs = [refs[4 * g:4 * g + 4] for g in range(group)]
    o_ref, m_ref, l_ref, acc_ref, car_ref = refs[4 * group:]
    step = pl.program_id(1)
    tq = SEQ_BLOCK
    tpad = tq // n_heads
    ps = knew_ref.shape[2]

    def process(kts, vts, biases, valid, first):
        ss = []
        for kt, bias_h in zip(kts, biases):
            bias = jnp.broadcast_to(bias_h[:, None, :], (n_heads, tpad, ps)).reshape(tq, ps)
            s = _dot(q_ref[0], kt.astype(BF16)) + bias
            ss.append(s if valid is None else jnp.where(valid, s, NEG_BIG))
        smax = functools.reduce(jnp.maximum, [jnp.max(s, axis=-1, keepdims=True) for s in ss])
        m_new = smax if first else jnp.maximum(m_ref[...], smax)
        ps_ = [jnp.exp(s - m_new) for s in ss]
        psum = sum(jnp.sum(p, axis=-1, keepdims=True) for p in ps_)
        pv = sum(_dot_nt(p.astype(BF16), vt.astype(BF16)) for p, vt in zip(ps_, vts))
        if first:
            acc_ref[...] = pv
            l_ref[...] = psum
        else:
            corr = jnp.exp(m_ref[...] - m_new)
            acc_ref[...] = acc_ref[...] * corr + pv
            l_ref[...] = l_ref[...] * corr + psum
        m_ref[...] = m_new

    @pl.when(step == 0)
    def _():
        t = lax.broadcasted_iota(jnp.int32, (tq, ps), 0) % tpad
        key = lax.broadcasted_iota(jnp.int32, (tq, ps), 1)
        process([knew_ref[0]], [vnew_ref[0]], [bnew_ref[0]], jnp.logical_and(key <= t, key < n_new), True)
        car_ref[...] = tnew_ref[0]

    @pl.when(step > 0)
    def _():
        car = car_ref[...]
        biases = []
        for _, _, suf_ref, tot_ref in page_refs:
            biases.append(suf_ref[0] + car)
            car = car + tot_ref[0]
        process([r[0][0] for r in page_refs], [r[1][0] for r in page_refs], biases, None, False)
        car_ref[...] = car

    @pl.when(step == pl.num_programs(1) - 1)
    def _():
        inv = 1.0 / l_ref[...]
        lo = lax.broadcasted_iota(jnp.int32, (tpad, LANES), 1) < HEAD_DIM
        for j in range(n_heads // 2):
            cols = slice(j * LANES, (j + 1) * LANES)
            ra = slice(2 * j * tpad, (2 * j + 1) * tpad)
            rb = slice((2 * j + 1) * tpad, (2 * j + 2) * tpad)
            o_ref[0, :, cols] = jnp.where(lo, acc_ref[ra, cols] * inv[ra], acc_ref[rb, cols] * inv[rb])


def _decode_attention(page_table, q, knew, vnew, bnew, tnew, ck, cv, suf, tot, *, n_new, n_heads):
    bd, n_pages = page_table.shape
    width, ps = ck.shape[1], ck.shape[2]
    tpad = SEQ_BLOCK // n_heads
    group = max(g for g in (PAGE_GROUP, 2, 1) if n_pages % g == 0)

    def page(g):
        return lambda i, s, pt: (pt[i, n_pages - 1 - g - (jnp.maximum(s, 1) - 1) * group], 0, 0)

    mine = lambda i, s, pt: (i, 0, 0)
    in_specs = [pl.BlockSpec((1, SEQ_BLOCK, width), mine),
                pl.BlockSpec((1, width, ps), mine),
                pl.BlockSpec((1, width, ps), mine),
                pl.BlockSpec((1, n_heads, ps), mine),
                pl.BlockSpec((1, n_heads, ps), mine)]
    args = [q, knew, vnew, bnew, tnew]
    for g in range(group):
        in_specs += [pl.BlockSpec((1, width, ps), page(g)), pl.BlockSpec((1, width, ps), page(g)),
                     pl.BlockSpec((1, n_heads, ps), page(g)), pl.BlockSpec((1, n_heads, ps), page(g))]
        args += [ck, cv, suf, tot]
    grid_spec = pltpu.PrefetchScalarGridSpec(
        num_scalar_prefetch=1,
        grid=(bd, n_pages // group + 1),
        in_specs=in_specs,
        out_specs=pl.BlockSpec((1, tpad, width), mine),
        scratch_shapes=[pltpu.VMEM((SEQ_BLOCK, 1), F32), pltpu.VMEM((SEQ_BLOCK, 1), F32),
                        pltpu.VMEM((SEQ_BLOCK, width), F32), pltpu.VMEM((n_heads, ps), F32)],
    )
    return pl.pallas_call(
        functools.partial(_decattn_kernel, n_new=n_new, n_heads=n_heads, group=group),
        grid_spec=grid_spec,
        out_shape=jax.ShapeDtypeStruct((bd, tpad, width), F32),
        compiler_params=_cparams("arbitrary", "arbitrary"),
        name="decattn",
    )(page_table, *args)


def _rwprep_kernel(*refs, seq_len, has_start):
    n_in = 4 + (4 if has_start else 0)
    p_refs = refs[0:4]
    st_refs = refs[4:8] if has_start else None
    (mur, muk, muv, mul, w0, w2, a0, a2, g2, kkw, kaw, rkw, g_ref) = refs[n_in:n_in + 13]
    (r_o, w_o, k_o, v_o, nkk_o, kka_o, gate_o, bon_o) = refs[n_in + 13:n_in + 21]
    carry = refs[n_in + 21:n_in + 25]
    i = pl.program_id(0)
    tm = p_refs[0].shape[0]

    @pl.when(i == 0)
    def _():
        for cr in carry:
            cr[...] = jnp.zeros_like(cr)

    mixed = []
    for n, (pr, mu) in enumerate(zip(p_refs, (mur, muk, muv, mul))):
        p = pr[...]
        row = lax.broadcasted_iota(jnp.int32, p.shape, 0)
        prev = jnp.where(row == 0, carry[n][0:1, :], pltpu.roll(p, 1, axis=0))
        if has_start:
            prev = jnp.where(row % seq_len == 0, st_refs[n][...], prev)
        carry[n][0:1, :] = p[tm - 1:tm, :]
        mixed.append(p + mu[...] * (prev - p))
    r, k, v, lora = mixed
    gm = g_ref[...]
    wa = lora[:, 0:LANES]
    wlog = _log_sigmoid(w0[...] + _dot(jnp.tanh(wa).astype(BF16), w2[...])) - 0.5
    decay = jnp.exp(-jnp.exp(wlog))
    a = jax.nn.sigmoid(a0[...] + _dot(wa.astype(BF16), a2[...]))
    gate = _dot(jax.nn.sigmoid(lora[:, LANES:]).astype(BF16), g2[...])
    kk = k * kkw[...]
    kk = kk * lax.rsqrt(_segsum(kk * kk, gm) + 1e-12)
    k2 = k * (1.0 + (a - 1.0) * kaw[...])
    r_o[...] = r
    w_o[...] = decay
    k_o[...] = k2
    v_o[...] = v
    nkk_o[...] = -kk
    kka_o[...] = kk * a
    gate_o[...] = gate
    bon_o[...] = _segsum(r * k2 * rkw[...], gm) * v


def _rwprep(p2, starts, params, gm, *, tm, seq_len, rcol, lcol):
    rows = p2.shape[0]
    wid = gm.shape[0]
    lw = 2 * LANES
    has_start = starts is not None
    row_spec = lambda w, cb: pl.BlockSpec((tm, w), lambda i: (i, cb))
    in_specs = [row_spec(wid, rcol), row_spec(wid, rcol + 1), row_spec(wid, rcol + 2), row_spec(lw, lcol)]
    args = [p2, p2, p2, p2]
    if has_start:
        in_specs += [row_spec(wid, 0), row_spec(wid, 0), row_spec(wid, 0), row_spec(lw, 0)]
        args += list(starts)
    for prm in params:
        in_specs.append(pl.BlockSpec(prm.shape, lambda i: (0, 0)))
    in_specs.append(pl.BlockSpec(gm.shape, lambda i: (0, 0)))
    args += list(params) + [gm]
    return pl.pallas_call(
        functools.partial(_rwprep_kernel, seq_len=seq_len, has_start=has_start),
        grid=(rows // tm,),
        in_specs=in_specs,
        out_specs=[pl.BlockSpec((tm, wid), lambda i: (i, 0))] * 8,
        out_shape=[jax.ShapeDtypeStruct((rows, wid), F32)] * 8,
        scratch_shapes=[pltpu.VMEM((8, wid), F32)] * 3 + [pltpu.VMEM((8, lw), F32)],
        compiler_params=_cparams("arbitrary"),
        name="rwprep",
    )(*args)


def _scan_kernel(r_ref, w_ref, k_ref, v_ref, a_ref, b_ref, s0_ref, y_ref, sT_ref, s_ref, *, tb):
    tblk = pl.program_id(1)
    dk = s_ref.shape[0]

    @pl.when(tblk == 0)
    def _():
        s_ref[...] = s0_ref[...]

    def step(t, carry):
        sa = s_ref[0] * a_ref[t, pl.ds(0, 1), :]
        for kx in range(1, dk):
            sa = sa + s_ref[kx] * a_ref[t, pl.ds(kx, 1), :]
        vt = v_ref[t]
        y = None
        for kx in range(dk):
            sk = (s_ref[kx] * w_ref[t, pl.ds(kx, 1), :] + sa * b_ref[t, pl.ds(kx, 1), :]
                  + vt * k_ref[t, pl.ds(kx, 1), :])
            s_ref[kx] = sk
            yk = sk * r_ref[t, pl.ds(kx, 1), :]
            y = yk if y is None else y + yk
        y_ref[t] = y
        return carry

    lax.fori_loop(0, tb, step, 0)

    @pl.when(tblk == pl.num_programs(1) - 1)
    def _():
        sT_ref[...] = s_ref[...]


def _scan(seqs, s0, *, tb, t_off, n_steps):
    npairs = s0.shape[-1]
    dk, dv = s0.shape[0], s0.shape[1]
    seq_spec = pl.BlockSpec((tb, dk, LANES), lambda g, t: (t + t_off // tb, 0, g))
    st_spec = pl.BlockSpec((dk, dv, LANES), lambda g, t: (0, 0, g))
    return pl.pallas_call(
        functools.partial(_scan_kernel, tb=tb),
        grid=(npairs // LANES, n_steps // tb),
        in_specs=[seq_spec] * 6 + [st_spec],
        out_specs=[pl.BlockSpec((tb, dv, LANES), lambda g, t: (t, 0, g)), st_spec],
        out_shape=[jax.ShapeDtypeStruct((n_steps, dv, npairs), F32), jax.ShapeDtypeStruct(s0.shape, F32)],
        scratch_shapes=[pltpu.VMEM((dk, dv, LANES), F32)],
        compiler_params=_cparams("arbitrary", "arbitrary"),
        name="wkvscan",
    )(*seqs, s0)


def _mix_kernel(x_ref, o_ref, og_ref, y_ref, gate_ref, bon_ref, on_ref, lnw_ref, lnb_ref, n2_ref,
                wo_ref, g_ref, h_ref, hn_ref):
    gm = g_ref[...]
    half = o_ref.shape[1]
    o = o_ref[...]
    fox = (o * lax.rsqrt(_segsum(o * o, gm) * (1.0 / HEAD_DIM) + RMS_EPS) * on_ref[...]
           * jax.nn.sigmoid(og_ref[...]))
    y = y_ref[...]
    d = y - _segsum(y, gm) * (1.0 / HEAD_DIM)
    var = _segsum(d * d, gm) * (1.0 / HEAD_DIM)
    rw = (d * lax.rsqrt(var + GN_EPS) * lnw_ref[...] + lnb_ref[...] + bon_ref[...]) * gate_ref[...]
    h = (x_ref[...] + _dot(fox.astype(BF16), wo_ref[0:half, :]) + _dot(rw.astype(BF16), wo_ref[half:, :]))
    h_ref[...] = h
    ms = jnp.mean(h * h, axis=-1, keepdims=True)
    hn_ref[...] = h * lax.rsqrt(ms + RMS_EPS) * n2_ref[...]


def _mix(x, o, og_src, y, gate_src, bon_src, vecs, wo, gm, *, n_blocks, og_map, rw_map):
    tm = SEQ_BLOCK
    d = x.shape[1]
    half = d // 2
    blk = lambda w, f: pl.BlockSpec((tm, w), f)
    const = lambda a: pl.BlockSpec(a.shape, lambda i: (0, 0))
    return pl.pallas_call(
        _mix_kernel,
        grid=(n_blocks,),
        in_specs=[blk(d, lambda i: (i, 0)), blk(half, lambda i: (i, 0)), blk(half, og_map),
                  blk(half, lambda i: (i, 0)), blk(half, rw_map), blk(half, rw_map)]
        + [const(v) for v in vecs] + [const(wo), const(gm)],
        out_specs=[blk(d, lambda i: (i, 0))] * 2,
        out_shape=[jax.ShapeDtypeStruct((n_blocks * tm, d), F32)] * 2,
        compiler_params=_cparams("arbitrary"),
        name="mix",
    )(x, o, og_src, y, gate_src, bon_src, *vecs, wo, gm)


def _extract_max(s_ref, t, out_v, out_i, idx_ref=None):
    n_rows = s_ref.shape[0]
    pos = lax.broadcasted_iota(jnp.int32, s_ref.shape, 0)
    s = s_ref[...]
    m = jnp.max(s, axis=0, keepdims=True)
    am = jnp.min(jnp.where(s == m, pos, n_rows), axis=0, keepdims=True)
    hit = pos == am
    out_v[pl.ds(t, 1), :] = m
    if idx_ref is None:
        out_i[pl.ds(t, 1), :] = am
    else:
        out_i[pl.ds(t, 1), :] = jnp.max(jnp.where(hit, idx_ref[...], -1), axis=0, keepdims=True)
    s_ref[...] = jnp.where(hit, -jnp.inf, s)


_CAND_KEEP = [PEER_TOPK // (i + 1) for i in range(PEER_TOPK)]
_CAND_ROWS = [-(-n // 8) * 8 for n in _CAND_KEEP]
_CAND_OFF = [sum(_CAND_ROWS[:i]) for i in range(PEER_TOPK)]
_N_CAND = sum(_CAND_ROWS)


def _peersel_kernel(hn_ref, wq_ref, keys_ref, eidx_ref, gate_ref, s_ref, sv_ref, si_ref, cand_ref, cidx_ref,
                    fv_ref, fi_ref):
    q = _dot(hn_ref[...].astype(BF16), wq_ref[...])
    nprob = 2 * PEER_HEADS
    for h in range(PEER_HEADS):
        qh = q[:, h * LANES:(h + 1) * LANES].astype(BF16)
        for half in range(2):
            s_ref[2 * h + half] = _dot_nt(keys_ref[h, half], qh)
    for t in range(PEER_TOPK):
        for n in range(nprob):
            _extract_max(s_ref.at[n], t, sv_ref.at[n], si_ref.at[n])
    for h in range(PEER_HEADS):
        for i in range(PEER_TOPK):
            nr = _CAND_ROWS[i]
            rows = slice(_CAND_OFF[i], _CAND_OFF[i] + nr)
            keep = lax.broadcasted_iota(jnp.int32, (nr, sv_ref.shape[2]), 0) < _CAND_KEEP[i]
            cand = sv_ref[2 * h, pl.ds(i, 1), :] + sv_ref[2 * h + 1, 0:nr, :]
            cand_ref[h, rows, :] = jnp.where(keep, cand, -jnp.inf)
            cidx_ref[h, rows, :] = si_ref[2 * h, pl.ds(i, 1), :] * PEER_N_KEYS + si_ref[2 * h + 1, 0:nr, :]
    for t in range(PEER_TOPK):
        for h in range(PEER_HEADS):
            _extract_max(cand_ref.at[h], t, fv_ref.at[h], fi_ref.at[h], idx_ref=cidx_ref.at[h])
    for h in range(PEER_HEADS):
        fv = fv_ref[h]
        e = jnp.exp(fv - fv[0:1, :])
        rows = slice(h * PEER_TOPK, (h + 1) * PEER_TOPK)
        gate_ref[rows, :] = e / jnp.sum(e, axis=0, keepdims=True)
        eidx_ref[rows, :] = fi_ref[h]


def _peer_select(hn, wq, keys):
    n, d = hn.shape
    tm = SEQ_BLOCK
    nsel = PEER_HEADS * PEER_TOPK
    nprob = 2 * PEER_HEADS
    return pl.pallas_call(
        _peersel_kernel,
        grid=(n // tm,),
        in_specs=[pl.BlockSpec((tm, d), lambda i: (i, 0)),
                  pl.BlockSpec(wq.shape, lambda i: (0, 0)),
                  pl.BlockSpec(keys.shape, lambda i: (0, 0, 0, 0))],
        out_specs=[pl.BlockSpec((nsel, tm), lambda i: (0, i))] * 2,
        out_shape=[jax.ShapeDtypeStruct((nsel, n), jnp.int32), jax.ShapeDtypeStruct((nsel, n), F32)],
        scratch_shapes=[pltpu.VMEM((nprob, PEER_N_KEYS, tm), F32),
                        pltpu.VMEM((nprob, PEER_TOPK, tm), F32), pltpu.VMEM((nprob, PEER_TOPK, tm), jnp.int32),
                        pltpu.VMEM((PEER_HEADS, _N_CAND, tm), F32), pltpu.VMEM((PEER_HEADS, _N_CAND, tm), jnp.int32),
                        pltpu.VMEM((PEER_HEADS, PEER_TOPK, tm), F32),
                        pltpu.VMEM((PEER_HEADS, PEER_TOPK, tm), jnp.int32)],
        compiler_params=_cparams("arbitrary"),
        name="peersel",
    )(hn, wq, keys)


def _peer_kernel(eidx_hbm, table_hbm, x_ref, gate_ref, hres_ref, out_ref, idx_smem, *scratch,
                 tb, nsel, rows_per_expert, nslot):
    bufs = scratch[0:nslot]
    part_ref, sem_idx, sems = scratch[nslot:]
    i = pl.program_id(0)
    rpe = rows_per_expert
    nchunk = rpe // 2
    slot_rows = nsel * rpe
    cp = pltpu.make_async_copy(eidx_hbm.at[pl.ds(pl.multiple_of(i * (tb * nsel), tb * nsel), tb * nsel)],
                               idx_smem, sem_idx)
    cp.start()
    cp.wait()

    grp = 8
    ngrp = nsel // grp

    def issue(t, slot):
        def body(g, carry):
            for j in range(grp):
                kx = g * grp + j
                e = idx_smem[t * nsel + kx]
                pltpu.make_async_copy(table_hbm.at[pl.ds(pl.multiple_of(e * rpe, rpe), rpe), :],
                                      bufs[slot].at[pl.ds(kx * rpe, rpe), :],
                                      sems.at[slot]).start(priority=j % 2)
            return carry

        for g in range(ngrp):
            body(g, 0)

    def wait(slot):
        pltpu.make_async_copy(table_hbm.at[pl.ds(0, slot_rows), :], bufs[slot], sems.at[slot]).wait()

    lane = lax.broadcasted_iota(jnp.int32, gate_ref.shape, 1)
    sub8 = lax.broadcasted_iota(jnp.int32, (8, LANES), 0)

    def fold(a, b, shift):
        low = (sub8 & shift) == 0
        return jnp.where(low, a + pltpu.roll(a, 8 - shift, axis=0), b + pltpu.roll(b, shift, axis=0))

    def sublane_sums(ts):
        ys = [fold(ts[j], ts[j + 4], 4) for j in range(4)]
        zs = [fold(ys[j], ys[j + 2], 2) for j in range(2)]
        return fold(zs[0], zs[1], 1)

    def compute(t, slot):
        xt = x_ref[t]

        def hidden(g, carry):
            base = g * (grp * rpe)
            ts = []
            for j in range(grp):
                p = bufs[slot][pl.ds(base + j * rpe, nchunk), :].astype(F32) * xt
                acc = p[0:8]
                for s in range(8, nchunk, 8):
                    acc = acc + p[s:s + 8]
                ts.append(acc)
            part_ref[pl.ds(g * grp, grp), :] = sublane_sums(ts)
            return carry

        for g in range(ngrp):
            hidden(g, 0)
        hid = jnp.sum(part_ref[...], axis=-1, keepdims=True)
        gate = jnp.sum(jnp.where(lane == t, gate_ref[...], 0.0), axis=-1, keepdims=True)
        act = gate * (hid * (lax.erf(hid / np.sqrt(2.0).astype(np.float32)) + 1.0) / 2.0)
        part_ref[...] = jnp.broadcast_to(act, (nsel, LANES))

        def combine(g, accs):
            base = g * (grp * rpe)
            a8 = part_ref[pl.ds(g * grp, grp), :]
            accs = list(accs)
            for j in range(grp):
                vrow = bufs[slot][pl.ds(base + j * rpe + nchunk, nchunk), :].astype(F32)
                accs[j % 4] = accs[j % 4] + vrow * a8[j:j + 1, :]
            return tuple(accs)

        zero = jnp.zeros((nchunk, LANES), F32)
        accs = (zero, zero, zero, zero)
        for g in range(ngrp):
            accs = combine(g, accs)
        out_ref[t] = hres_ref[t] + ((accs[0] + accs[1]) + (accs[2] + accs[3]))

    for t in range(nslot - 1):
        issue(t, t)

    def body(j, carry):
        base = j * nslot
        for sub in range(nslot):
            wait(sub)
            issue(jnp.minimum(base + sub + nslot - 1, tb - 1), (sub + nslot - 1) % nslot)
            compute(base + sub, sub)
        return carry

    lax.fori_loop(0, tb // nslot, body, 0)
    for slot in range(nslot - 1):
        wait(slot)


def _peer_combine(eidx_flat, table, hn, gate_t, hres, *, nsel, rows_per_expert):
    n, d = hn.shape
    tb = SEQ_BLOCK
    nslot = 4
    nchunk = d // LANES
    tok = pl.BlockSpec((tb, nchunk, LANES), lambda i: (i, 0, 0))
    out = pl.pallas_call(
        functools.partial(_peer_kernel, tb=tb, nsel=nsel, rows_per_expert=rows_per_expert, nslot=nslot),
        grid=(n // tb,),
        in_specs=[pl.BlockSpec(memory_space=pl.ANY),
                  pl.BlockSpec(memory_space=pl.ANY),
                  tok,
                  pl.BlockSpec((nsel, tb), lambda i: (0, i)),
                  tok],
        out_specs=tok,
        out_shape=jax.ShapeDtypeStruct((n, nchunk, LANES), F32),
        scratch_shapes=[pltpu.SMEM((tb * nsel,), jnp.int32),
                        *[pltpu.VMEM((nsel * rows_per_expert, LANES), table.dtype) for _ in range(nslot)],
                        pltpu.VMEM((nsel, LANES), F32),
                        pltpu.SemaphoreType.DMA(()),
                        pltpu.SemaphoreType.DMA((nslot,))],
        compiler_params=_cparams("arbitrary"),
        name="peer",
    )(eidx_flat, table, hn.reshape(n, nchunk, LANES), gate_t, hres.reshape(n, nchunk, LANES))
    return out.reshape(n, d)


def _to_pairs(a, b, l):
    nh = a.shape[1] // HEAD_DIM
    a = a.reshape(b, l, nh, HEAD_DIM).transpose(1, 3, 0, 2).reshape(l, HEAD_DIM, b * nh)
    pad = (-a.shape[-1]) % LANES
    return jnp.pad(a, ((0, 0), (0, 0), (0, pad))) if pad else a


def kernel(x_prompt, x_sample, cache_k, cache_v, cache_logf, state_wkv, state_shift, page_table, meta_tokens,
           norm1_w, w_in, fox_bf, fox_qn, fox_kn, fox_on, rw_mu, rw_w0, rw_w2, rw_a0, rw_a2, rw_g2, rw_kk,
           rw_ka, rw_rk, rw_ln_w, rw_ln_b, w_o, norm2_w, peer_wq, peer_keys, peer_u, peer_v):
    depth = w_in.shape[0]
    assert depth == 1, "single-layer step"
    bsz, seq, d = x_prompt.shape
    bd, tdec, _ = x_sample.shape
    half = d // 2
    nh = half // HEAD_DIM
    nhp = half // LANES
    lp = FRONT_PAD + N_META + seq
    nq = seq // SEQ_BLOCK
    lora_w = 2 * LANES
    assert seq % SEQ_BLOCK == 0 and rw_w2.shape[1] + rw_a2.shape[1] == LANES and rw_g2.shape[1] == LANES

    tn = lora_w
    w = w_in[0]
    fox_cols = 4 * half + nh
    wq_, wk_, wv_ = w[:, 0:half], w[:, half:2 * half], w[:, 2 * half:3 * half]
    wfl, wog = w[:, 3 * half:3 * half + nh], w[:, 3 * half + nh:fox_cols]
    wrw = w[:, fox_cols:]
    w_all = jnp.concatenate([wq_, wk_, wv_, wog, wrw, jnp.pad(wfl, ((0, 0), (0, tn - nh)))], axis=1).astype(BF16)
    ncol = w_all.shape[1]
    nblk = ncol // tn
    n_norm = 2 * half // tn
    j_fl = nblk - 1
    rep = tn // HEAD_DIM
    cvec = jnp.zeros((nblk, tn), F32)
    cvec = cvec.at[0:n_norm // 2].set(jnp.tile(fox_qn[0] * (HEAD_DIM ** -0.5), rep))
    cvec = cvec.at[n_norm // 2:n_norm].set(jnp.tile(fox_kn[0], rep))
    cvec = cvec.at[j_fl, 0:nh].set(fox_bf[0]).reshape(nblk, 1, tn)
    g_tn = _head_ones(tn)
    g_half = _head_ones(half)
    nw1 = norm1_w[0].reshape(1, d)
    col_k, col_v, col_og, col_rw = half, 2 * half, 3 * half, 4 * half
    col_lora = col_rw + 3 * half
    col_fl = col_lora + lora_w

    mu = rw_mu[0]
    row = lambda a: a.reshape(1, -1)
    zpad = jnp.zeros((LANES - rw_w2.shape[1], half), F32)
    rw_params = [row(mu[0:half]), row(mu[half:2 * half]), row(mu[2 * half:3 * half]), row(mu[3 * half:]),
                 row(rw_w0[0]), jnp.concatenate([rw_w2[0], zpad], 0).astype(BF16),
                 row(rw_a0[0]), jnp.concatenate([zpad, rw_a2[0]], 0).astype(BF16),
                 rw_g2[0].astype(BF16), row(rw_kk[0]), row(rw_ka[0]), row(rw_rk[0])]
    mix_vecs = [row(fox_on[0]), row(rw_ln_w[0]), row(rw_ln_b[0]), row(norm2_w[0])]
    wo = w_o[0].astype(BF16)
    wq_peer = peer_wq[0].astype(BF16)
    khalf = peer_keys.shape[-1]
    keys = peer_keys[0].astype(BF16)
    keys = jnp.stack([jnp.pad(keys[:, 0], ((0, 0), (0, 0), (0, LANES - khalf))),
                      jnp.pad(keys[:, 1], ((0, 0), (0, 0), (khalf, LANES - 2 * khalf))) ], axis=1)
    rpe = 2 * d // LANES
    table = jnp.concatenate([peer_u[0], peer_v[0]], axis=1).astype(BF16).reshape(-1, LANES)
    nsel = PEER_HEADS * PEER_TOPK

    def peer(hn, hres):
        eidx_t, gate_t = _peer_select(hn, wq_peer, keys)
        return _peer_combine(eidx_t.T.reshape(-1), table, hn, gate_t, hres, nsel=nsel, rows_per_expert=rpe)

    hp = jnp.concatenate([jnp.zeros((bsz, FRONT_PAD, d), F32),
                          jnp.broadcast_to(meta_tokens.astype(F32)[None], (bsz, N_META, d)),
                          x_prompt.astype(F32)], axis=1).reshape(bsz * lp, d)
    tm_p = 1024 if (bsz * lp) % 1024 == 0 else SEQ_BLOCK
    pp = _inproj(hp, nw1, w_all, cvec, g_tn, tm=tm_p, tn=tn, n_norm=n_norm, j_fl=j_fl)
    pp3 = pp.reshape(bsz, lp, ncol)
    ccol, crow = _fcum(pp3, col_fl // LANES)
    ccol = ccol[:, :, 0:nh].reshape(bsz, lp, nhp, 2).transpose(0, 2, 1, 3)
    crow = crow[:, 0:nh, :].reshape(bsz, nhp, 2, lp)
    o_p = _attention(pp3, ccol, crow, nq=nq, kcol=col_k // LANES, vcol=col_v // LANES)
    o_p = o_p.reshape(bsz * seq, half)

    rw_p = _rwprep(pp, None, rw_params, g_half, tm=SEQ_BLOCK, seq_len=lp,
                   rcol=col_rw // half, lcol=col_lora // lora_w)
    seqs_p = [_to_pairs(a, bsz, lp) for a in rw_p[0:6]]
    npair_p = seqs_p[0].shape[-1]
    tb_p = N_META
    y_p, s_p = _scan(seqs_p, jnp.zeros((HEAD_DIM, HEAD_DIM, npair_p), F32),
                     tb=tb_p, t_off=FRONT_PAD, n_steps=N_META + seq)
    y_p = y_p[N_META:, :, 0:bsz * nh].reshape(seq, HEAD_DIM, bsz, nh).transpose(2, 0, 3, 1).reshape(bsz * seq, half)
    wkv_p = s_p[:, :, 0:bsz * nh].transpose(2, 1, 0).reshape(1, bsz, nh, HEAD_DIM, HEAD_DIM)

    blocks_per_seq = lp // SEQ_BLOCK
    in_rows = lambda i: (i // nq) * blocks_per_seq + 1 + i % nq
    h_p, hn_p = _mix(x_prompt.reshape(bsz * seq, d).astype(F32), o_p, pp, y_p, rw_p[6], rw_p[7], mix_vecs, wo,
                     g_half, n_blocks=bsz * nq,
                     og_map=lambda i: (in_rows(i), col_og // half), rw_map=lambda i: (in_rows(i), 0))
    y_prompt = peer(hn_p, h_p).reshape(bsz, seq, d)

    pr = pp3[:, FRONT_PAD:]
    k_prompt = pr[:, :, col_k:col_k + half].reshape(1, bsz, N_META + seq, nh, HEAD_DIM)
    v_prompt = pr[:, :, col_v:col_v + half].reshape(1, bsz, N_META + seq, nh, HEAD_DIM)
    logf_prompt = pr[:, :, col_fl:col_fl + nh].reshape(1, bsz, N_META + seq, nh)
    shift_prompt = pp3[:, -1, col_rw:col_lora + lora_w].reshape(1, bsz, -1)

    nrow_d = bd * tdec
    rows_d = -(-nrow_d // SEQ_BLOCK) * SEQ_BLOCK
    xs = x_sample.astype(F32).reshape(nrow_d, d)
    xs_pad = jnp.pad(xs, ((0, rows_d - nrow_d), (0, 0)))
    pd_ = _inproj(xs_pad, nw1, w_all, cvec, g_tn, tm=SEQ_BLOCK, tn=tn, n_norm=n_norm, j_fl=j_fl)
    pd3 = pd_[0:nrow_d].reshape(bd, tdec, ncol)
    tpad = SEQ_BLOCK // nh
    n_pool, page_size = cache_k.shape[1], cache_k.shape[2]
    pages = lambda c: c[0].astype(F32).transpose(0, 2, 3, 1).reshape(n_pool, half, page_size)
    q4 = pd3[:, :, 0:half].reshape(bd, tdec, nh, HEAD_DIM)
    q_d = jnp.eye(nh, dtype=F32)[None, :, None, :, None] * q4[:, None]
    q_d = jnp.pad(q_d, ((0, 0), (0, 0), (0, tpad - tdec), (0, 0), (0, 0))).reshape(bd, SEQ_BLOCK, half).astype(BF16)
    padk = ((0, 0), (0, 0), (0, page_size - tdec))
    knew = jnp.pad(pd3[:, :, col_k:col_k + half].transpose(0, 2, 1), padk)
    vnew = jnp.pad(pd3[:, :, col_v:col_v + half].transpose(0, 2, 1), padk)
    lnew = jnp.pad(pd3[:, :, col_fl:col_fl + nh].transpose(0, 2, 1), padk)
    bnew, tnew = _page_suffix(lnew)
    suf, tot = _page_suffix(cache_logf[0].astype(F32).transpose(0, 2, 1))
    o_d = _decode_attention(page_table, q_d, knew, vnew, bnew, tnew, pages(cache_k), pages(cache_v), suf, tot,
                            n_new=tdec, n_heads=nh)
    o_d = jnp.pad(o_d[:, 0:tdec].reshape(nrow_d, half), ((0, rows_d - nrow_d), (0, 0)))

    sh = state_shift[0].astype(F32)
    st = jnp.pad(jnp.repeat(sh, tdec, axis=0), ((0, rows_d - nrow_d), (0, 0)))
    starts = [st[:, 0:half], st[:, half:2 * half], st[:, 2 * half:3 * half], st[:, 3 * half:]]
    rw_d = _rwprep(pd_, starts, rw_params, g_half, tm=SEQ_BLOCK, seq_len=tdec,
                   rcol=col_rw // half, lcol=col_lora // lora_w)
    seqs_d = [_to_pairs(a[0:nrow_d], bd, tdec) for a in rw_d[0:6]]
    s0_d = state_wkv[0].astype(F32).reshape(bd * nh, HEAD_DIM, HEAD_DIM).transpose(2, 1, 0)
    pad_d = (-s0_d.shape[-1]) % LANES
    if pad_d:
        s0_d = jnp.pad(s0_d, ((0, 0), (0, 0), (0, pad_d)))
    y_d, s_d = _scan(seqs_d, s0_d, tb=tdec, t_off=0, n_steps=tdec)
    y_d = y_d[:, :, 0:bd * nh].reshape(tdec, HEAD_DIM, bd, nh).transpose(2, 0, 3, 1).reshape(nrow_d, half)
    y_d = jnp.pad(y_d, ((0, rows_d - nrow_d), (0, 0)))
    wkv_d = s_d[:, :, 0:bd * nh].transpose(2, 1, 0).reshape(1, bd, nh, HEAD_DIM, HEAD_DIM)
    h_d, hn_d = _mix(xs_pad, o_d, pd_, y_d, rw_d[6], rw_d[7], mix_vecs, wo, g_half, n_blocks=rows_d // SEQ_BLOCK,
                     og_map=lambda i: (i, col_og // half), rw_map=lambda i: (i, 0))
    y_sample = peer(hn_d, h_d)[0:nrow_d].reshape(bd, tdec, d)

    k_sample = pd3[:, :, col_k:col_k + half].reshape(1, bd, tdec, nh, HEAD_DIM)
    v_sample = pd3[:, :, col_v:col_v + half].reshape(1, bd, tdec, nh, HEAD_DIM)
    logf_sample = pd3[:, :, col_fl:col_fl + nh].reshape(1, bd, tdec, nh)
    shift_sample = pd3[:, -1, col_rw:col_lora + lora_w].reshape(1, bd, -1)

    return (y_prompt, y_sample, k_prompt, v_prompt, logf_prompt, wkv_p, shift_prompt,
            k_sample, v_sample, logf_sample, wkv_d, shift_sample)
```
